```python
import jax, jax.numpy as jnp
from jax import lax
import numpy as np

D_MODEL = 1024
BATCH = 4
SEQ = 8192
DEPTH = 4

N_MIXERS = 2
POOL_WINDOWS = (2, 4, 8, 16)
N_POOL_GROUPS = len(POOL_WINDOWS)
POOL_GROUP_DIM = D_MODEL // N_POOL_GROUPS
CONV_KERNEL = 31
D_FF = ((8 * D_MODEL // 3 + 255) // 256) * 256
RMS_EPS = 1e-6
LN_EPS = 1e-5
FFN_RESIDUAL_WEIGHT = 0.5
N_POOL_LAYERS = len(range(0, DEPTH, N_MIXERS))
N_CONV_LAYERS = len(range(1, DEPTH, N_MIXERS))

kernel_name = "hybrid_pool_conformer_macaron_encoder"


def rmsnorm(x, g):
    xf = x.astype(jnp.float32)
    y = xf * lax.rsqrt(jnp.mean(xf * xf, axis=-1, keepdims=True) + RMS_EPS)
    return (y * g.astype(jnp.float32)).astype(x.dtype)


def layernorm(x, g, b):
    xf = x.astype(jnp.float32)
    mu = jnp.mean(xf, axis=-1, keepdims=True)
    xc = xf - mu
    var = jnp.mean(xc * xc, axis=-1, keepdims=True)
    y = xc * lax.rsqrt(var + LN_EPS) * g.astype(jnp.float32) + b.astype(jnp.float32)
    return y.astype(x.dtype)


def swiglu_ffn(x, w_in, w_out):
    gate, up = jnp.split(x @ w_in, 2, axis=-1)
    return (jax.nn.silu(gate) * up) @ w_out


def pool_mixer(x, w_pool, b_pool, scale):
    B, S, D = x.shape
    xf = x.astype(jnp.float32)
    c = jnp.concatenate([jnp.zeros((B, 1, D), jnp.float32), jnp.cumsum(xf, axis=1)], axis=1)
    t = jnp.arange(S)
    parts = []
    for g, w in enumerate(POOL_WINDOWS):
        cg = c[..., g * POOL_GROUP_DIM:(g + 1) * POOL_GROUP_DIM]
        lo = jnp.clip(t - w // 2, 0, S)
        hi = jnp.clip(t - w // 2 + w, 0, S)
        cnt = (hi - lo).astype(jnp.float32)
        win_sum = jnp.take(cg, hi, axis=1) - jnp.take(cg, lo, axis=1)
        parts.append(win_sum / cnt[None, :, None])
    pooled = jnp.concatenate(parts, axis=-1) - xf
    pooled = pooled.astype(x.dtype).reshape(B, S, N_POOL_GROUPS, POOL_GROUP_DIM)
    y = jnp.einsum('bsgc,gcd->bsgd', pooled, w_pool).reshape(B, S, D) + b_pool
    return y * scale


def conformer_conv(x, w_pw1, b_pw1, w_dw, b_dw, ln_g, ln_b, w_pw2, b_pw2):
    D = x.shape[-1]
    a, gate = jnp.split(x @ w_pw1 + b_pw1, 2, axis=-1)
    h = a * jax.nn.sigmoid(gate)
    h = lax.conv_general_dilated(
        h, w_dw[:, None, :], window_strides=(1,),
        padding=[(CONV_KERNEL // 2, CONV_KERNEL // 2)],
        dimension_numbers=('NWC', 'WIO', 'NWC'),
        feature_group_count=D) + b_dw
    h = jax.nn.silu(layernorm(h, ln_g, ln_b))
    return h @ w_pw2 + b_pw2


def setup_inputs(seed: int = 0) -> dict:
    key = jax.random.key(seed)
    ks = jax.random.split(key, 16)
    D, F, K = D_MODEL, D_FF, CONV_KERNEL
    nrm = lambda k, shape, s: jax.random.normal(k, shape, jnp.float32) * s
    return {
        "x": nrm(ks[0], (BATCH, SEQ, D), 1.0),
        "norm_g": 1.0 + nrm(ks[1], (DEPTH, 3, D), 0.05),
        "ffn_w_in": nrm(ks[2], (DEPTH, 2, D, 2 * F), D ** -0.5),
        "ffn_w_out": nrm(ks[3], (DEPTH, 2, F, D), F ** -0.5),
        "pool_w": nrm(ks[4], (N_POOL_LAYERS, N_POOL_GROUPS, POOL_GROUP_DIM, POOL_GROUP_DIM), POOL_GROUP_DIM ** -0.5),
        "pool_b": nrm(ks[5], (N_POOL_LAYERS, D), 0.02),
        "pool_scale": 1.0 + nrm(ks[6], (N_POOL_LAYERS, D), 0.1),
        "conv_w_pw1": nrm(ks[7], (N_CONV_LAYERS, D, 2 * D), D ** -0.5),
        "conv_b_pw1": nrm(ks[8], (N_CONV_LAYERS, 2 * D), 0.02),
        "conv_w_dw": nrm(ks[9], (N_CONV_LAYERS, K, D), K ** -0.5),
        "conv_b_dw": nrm(ks[10], (N_CONV_LAYERS, D), 0.02),
        "conv_ln_g": 1.0 + nrm(ks[11], (N_CONV_LAYERS, D), 0.05),
        "conv_ln_b": nrm(ks[12], (N_CONV_LAYERS, D), 0.02),
        "conv_w_pw2": nrm(ks[13], (N_CONV_LAYERS, D, D), D ** -0.5),
        "conv_b_pw2": nrm(ks[14], (N_CONV_LAYERS, D), 0.02),
        "final_g": 1.0 + nrm(ks[15], (D,), 0.05),
    }


def reference(x, norm_g, ffn_w_in, ffn_w_out, pool_w, pool_b, pool_scale,
              conv_w_pw1, conv_b_pw1, conv_w_dw, conv_b_dw, conv_ln_g, conv_ln_b,
              conv_w_pw2, conv_b_pw2, final_g):
    for i in range(DEPTH):
        x = x + FFN_RESIDUAL_WEIGHT * swiglu_ffn(rmsnorm(x, norm_g[i, 0]), ffn_w_in[i, 0], ffn_w_out[i, 0])
        h = rmsnorm(x, norm_g[i, 1])
        j = i // N_MIXERS
        if i % N_MIXERS == 0:
            x = x + pool_mixer(h, pool_w[j], pool_b[j], pool_scale[j])
        else:
            x = x + conformer_conv(h, conv_w_pw1[j], conv_b_pw1[j], conv_w_dw[j], conv_b_dw[j],
                                   conv_ln_g[j], conv_ln_b[j], conv_w_pw2[j], conv_b_pw2[j])
        x = x + FFN_RESIDUAL_WEIGHT * swiglu_ffn(rmsnorm(x, norm_g[i, 2]), ffn_w_in[i, 1], ffn_w_out[i, 1])
    return rmsnorm(x, final_g)
```

```python
import functools

import jax
import jax.numpy as jnp
from jax import lax
from jax.experimental import pallas as pl
from jax.experimental.pallas import tpu as pltpu

POOL_WINDOWS = (2, 4, 8, 16)
N_MIXERS = 2
RMS_EPS = 1e-6
LN_EPS = 1e-5
FFN_RESIDUAL_WEIGHT = 0.5

V7X_MXU_DIM = 256
V7X_BF16_SUBLANES = 16
V7X_VMEM_BYTES = 64 * 2**20

FFN_TOKEN_TILE = 512
FFN_HIDDEN_CHUNK = V7X_MXU_DIM
MIXER_TOKEN_TILE = 512
POOL_HALO = 8
CONV_HALO = V7X_BF16_SUBLANES
CONV_ROW_CHUNK = 32

BF16 = jnp.bfloat16
F32 = jnp.float32


def _rmsnorm(x, g):
    return x * lax.rsqrt(jnp.mean(x * x, axis=-1, keepdims=True) + RMS_EPS) * g


def _vmem_limit(resident_bytes):
    return int(min(resident_bytes + 16 * 2**20, V7X_VMEM_BYTES - 8 * 2**20))


def _ffn_kernel(x_ref, g_ref, win_ref, wout_ref, o_ref, act_ref, *, d_ff, chunk):
    x = x_ref[...]
    h = _rmsnorm(x, g_ref[...]).astype(BF16)
    for c in range(d_ff // chunk):
        lo = c * chunk
        gate = jnp.dot(h, win_ref[:, lo:lo + chunk], preferred_element_type=F32)
        up = jnp.dot(h, win_ref[:, d_ff + lo:d_ff + lo + chunk], preferred_element_type=F32)
        act_ref[:, lo:lo + chunk] = (gate * jax.nn.sigmoid(gate) * up).astype(BF16)
    y = jnp.dot(act_ref[...], wout_ref[...], preferred_element_type=F32)
    o_ref[...] = x + FFN_RESIDUAL_WEIGHT * y


def _ffn(x2d, norm_g, w_in, w_out, layer, which, norm_row):
    n_tok, d = x2d.shape
    d_ff = w_out.shape[2]
    tm = FFN_TOKEN_TILE
    assert n_tok % tm == 0 and d_ff % FFN_HIDDEN_CHUNK == 0
    resident = (2 * 2 * tm * d * 4) + 2 * (w_in[0, 0].size + w_out[0, 0].size) * 2 + tm * d_ff * 2
    return pl.pallas_call(
        functools.partial(_ffn_kernel, d_ff=d_ff, chunk=FFN_HIDDEN_CHUNK),
        grid=(n_tok // tm,),
        in_specs=[
            pl.BlockSpec((tm, d), lambda t: (t, 0)),
            pl.BlockSpec((None, 1, d), lambda t: (norm_row, 0, 0)),
            pl.BlockSpec((None, None, d, 2 * d_ff), lambda t: (layer, which, 0, 0)),
            pl.BlockSpec((None, None, d_ff, d), lambda t: (layer, which, 0, 0)),
        ],
        out_specs=pl.BlockSpec((tm, d), lambda t: (t, 0)),
        out_shape=jax.ShapeDtypeStruct((n_tok, d), x2d.dtype),
        scratch_shapes=[pltpu.VMEM((tm, d_ff), BF16)],
        compiler_params=pltpu.CompilerParams(
            dimension_semantics=("arbitrary",), vmem_limit_bytes=_vmem_limit(resident)),
        name=f"ffn_l{layer}_{which}",
    )(x2d, norm_g, w_in, w_out)


def _pool_kernel(xp_ref, x_ref, xn_ref, g_ref, w_ref, b_ref, s_ref, o_ref, h_ref, *, seq_len, tile):
    i = pl.program_id(1)
    last = pl.num_programs(1) - 1
    g = g_ref[...]
    halo = POOL_HALO
    h_ref[0:halo, :] = jnp.where(i > 0, _rmsnorm(xp_ref[...], g), 0.0)
    h_ref[halo:halo + tile, :] = _rmsnorm(x_ref[...], g)
    h_ref[halo + tile:, :] = jnp.where(i < last, _rmsnorm(xn_ref[...], g), 0.0)

    t = i * tile + lax.broadcasted_iota(jnp.int32, (tile, 1), 0)
    gd = w_ref.shape[-1]
    for gi, w in enumerate(POOL_WINDOWS):
        lanes = slice(gi * gd, (gi + 1) * gd)
        start = halo - w // 2
        win_sum = h_ref[start:start + tile, lanes]
        for k in range(1, w):
            win_sum = win_sum + h_ref[start + k:start + k + tile, lanes]
        cnt = (jnp.clip(t - w // 2 + w, 0, seq_len) - jnp.clip(t - w // 2, 0, seq_len)).astype(F32)
        pooled = win_sum / cnt - h_ref[halo:halo + tile, lanes]
        y = jnp.dot(pooled.astype(BF16), w_ref[gi], preferred_element_type=F32) + b_ref[:, lanes]
        o_ref[:, lanes] = x_ref[:, lanes] + y * s_ref[:, lanes]


def _pool_mixer(x, norm_g, norm_row, pool_w, pool_b, pool_scale, j):
    b, s, d = x.shape
    tile = MIXER_TOKEN_TILE
    halo = POOL_HALO
    assert s % tile == 0 and tile % halo == 0
    tiles_per_halo = tile // halo
    n_halo_blocks = s // halo
    row_spec = lambda arr_row: pl.BlockSpec((None, 1, d), lambda bi, i: (arr_row, 0, 0))
    return pl.pallas_call(
        functools.partial(_pool_kernel, seq_len=s, tile=tile),
        grid=(b, s // tile),
        in_specs=[
            pl.BlockSpec((None, halo, d), lambda bi, i: (bi, jnp.maximum(i * tiles_per_halo - 1, 0), 0)),
            pl.BlockSpec((None, tile, d), lambda bi, i: (bi, i, 0)),
            pl.BlockSpec((None, halo, d),
                         lambda bi, i: (bi, jnp.minimum((i + 1) * tiles_per_halo, n_halo_blocks - 1), 0)),
            row_spec(norm_row),
            pl.BlockSpec((None,) + pool_w.shape[1:], lambda bi, i: (j, 0, 0, 0)),
            row_spec(j),
            row_spec(j),
        ],
        out_specs=pl.BlockSpec((None, tile, d), lambda bi, i: (bi, i, 0)),
        out_shape=jax.ShapeDtypeStruct(x.shape, x.dtype),
        scratch_shapes=[pltpu.VMEM((tile + 2 * halo, d), F32)],
        compiler_params=pltpu.CompilerParams(dimension_semantics=("arbitrary", "arbitrary")),
        name=f"pool_mixer_{j}",
    )(x, x, x, norm_g, pool_w, pool_b, pool_scale)


def _conv_kernel(xp_ref, x_ref, xn_ref, g_ref, w1_ref, b1_ref, wdw_ref, bdw_ref, lng_ref, lnb_ref,
                 w2_ref, b2_ref, o_ref, h_ref, glu_ref, act_ref, *, tile, n_taps):
    i = pl.program_id(1)
    last = pl.num_programs(1) - 1
    g = g_ref[...]
    halo = CONV_HALO
    d = x_ref.shape[-1]
    h_ref[0:halo, :] = _rmsnorm(xp_ref[...], g).astype(BF16)
    h_ref[halo:halo + tile, :] = _rmsnorm(x_ref[...], g).astype(BF16)
    h_ref[halo + tile:, :] = _rmsnorm(xn_ref[...], g).astype(BF16)

    z = jnp.dot(h_ref[...], w1_ref[...], preferred_element_type=F32) + b1_ref[...]
    glu = z[:, :d] * jax.nn.sigmoid(z[:, d:])
    row = lax.broadcasted_iota(jnp.int32, (tile + 2 * halo, 1), 0)
    inside = jnp.logical_and(jnp.logical_or(row >= halo, i > 0),
                             jnp.logical_or(row < halo + tile, i < last))
    glu_ref[...] = jnp.where(inside, glu, 0.0)

    first = halo - n_taps // 2
    rc = CONV_ROW_CHUNK

    for r in range(tile // rc):
        r0 = r * rc
        acc = jnp.broadcast_to(bdw_ref[...], (rc, d))
        for k in range(n_taps):
            acc = acc + glu_ref[r0 + first + k:r0 + first + k + rc, :] * wdw_ref[k:k + 1, :]
        mu = jnp.mean(acc, axis=-1, keepdims=True)
        xc = acc - mu
        var = jnp.mean(xc * xc, axis=-1, keepdims=True)
        v = xc * lax.rsqrt(var + LN_EPS) * lng_ref[...] + lnb_ref[...]
        act_ref[r0:r0 + rc, :] = (v * jax.nn.sigmoid(v)).astype(BF16)
    y = jnp.dot(act_ref[...], w2_ref[...], preferred_element_type=F32) + b2_ref[...]
    o_ref[...] = x_ref[...] + y


def _conv_mixer(x, norm_g, norm_row, w_pw1, b_pw1, w_dw, b_dw, ln_g, ln_b, w_pw2, b_pw2, j):
    b, s, d = x.shape
    tile = MIXER_TOKEN_TILE
    halo = CONV_HALO
    n_taps = w_dw.shape[1]
    assert s % tile == 0 and tile % halo == 0 and n_taps // 2 <= halo and tile % CONV_ROW_CHUNK == 0
    tiles_per_halo = tile // halo
    n_halo_blocks = s // halo
    row_spec = lambda arr_row, width: pl.BlockSpec((None, 1, width), lambda bi, i: (arr_row, 0, 0))
    resident = 2 * 2 * (w_pw1[0].size + w_pw2[0].size) + 4 * 2 * tile * d * 4 + 4 * (tile + 2 * halo) * d * 4
    return pl.pallas_call(
        functools.partial(_conv_kernel, tile=tile, n_taps=n_taps),
        grid=(b, s // tile),
        in_specs=[
            pl.BlockSpec((None, halo, d), lambda bi, i: (bi, jnp.maximum(i * tiles_per_halo - 1, 0), 0)),
            pl.BlockSpec((None, tile, d), lambda bi, i: (bi, i, 0)),
            pl.BlockSpec((None, halo, d),
                         lambda bi, i: (bi, jnp.minimum((i + 1) * tiles_per_halo, n_halo_blocks - 1), 0)),
            row_spec(norm_row, d),
            pl.BlockSpec((None, d, 2 * d), lambda bi, i: (j, 0, 0)),
            row_spec(j, 2 * d),
            pl.BlockSpec((None, n_taps, d), lambda bi, i: (j, 0, 0)),
            row_spec(j, d),
            row_spec(j, d),
            row_spec(j, d),
            pl.BlockSpec((None, d, d), lambda bi, i: (j, 0, 0)),
            row_spec(j, d),
        ],
        out_specs=pl.BlockSpec((None, tile, d), lambda bi, i: (bi, i, 0)),
        out_shape=jax.ShapeDtypeStruct(x.shape, x.dtype),
        scratch_shapes=[
            pltpu.VMEM((tile + 2 * halo, d), BF16),
            pltpu.VMEM((tile + 2 * halo, d), F32),
            pltpu.VMEM((tile, d), BF16),
        ],
        compiler_params=pltpu.CompilerParams(
            dimension_semantics=("arbitrary", "arbitrary"), vmem_limit_bytes=_vmem_limit(resident)),
        name=f"conv_mixer_{j}",
    )(x, x, x, norm_g, w_pw1, b_pw1, w_dw, b_dw, ln_g, ln_b, w_pw2, b_pw2)


def _final_norm_kernel(x_ref, g_ref, o_ref):
    o_ref[...] = _rmsnorm(x_ref[...], g_ref[...])


def _final_norm(x2d, g):
    n_tok, d = x2d.shape
    tm = FFN_TOKEN_TILE
    return pl.pallas_call(
        _final_norm_kernel,
        grid=(n_tok // tm,),
        in_specs=[pl.BlockSpec((tm, d), lambda t: (t, 0)), pl.BlockSpec((1, d), lambda t: (0, 0))],
        out_specs=pl.BlockSpec((tm, d), lambda t: (t, 0)),
        out_shape=jax.ShapeDtypeStruct(x2d.shape, x2d.dtype),
        compiler_params=pltpu.CompilerParams(dimension_semantics=("arbitrary",)),
        name="final_norm",
    )(x2d, g)


def kernel(x, norm_g, ffn_w_in, ffn_w_out, pool_w, pool_b, pool_scale, conv_w_pw1, conv_b_pw1, conv_w_dw,
           conv_b_dw, conv_ln_g, conv_ln_b, conv_w_pw2, conv_b_pw2, final_g):
    b, s, d = x.shape
    depth = norm_g.shape[0]
    row3 = lambda a: a.reshape(a.shape[0], 1, a.shape[1])
    norm_rows = norm_g.reshape(depth * 3, 1, d)
    w_in = ffn_w_in.astype(BF16)
    w_out = ffn_w_out.astype(BF16)
    pool_w16 = pool_w.astype(BF16)
    w_pw1 = conv_w_pw1.astype(BF16)
    w_pw2 = conv_w_pw2.astype(BF16)

    for i in range(depth):
        x = _ffn(x.reshape(b * s, d), norm_rows, w_in, w_out, i, 0, 3 * i).reshape(b, s, d)
        j = i // N_MIXERS
        if i % N_MIXERS == 0:
            x = _pool_mixer(x, norm_rows, 3 * i + 1, pool_w16, row3(pool_b), row3(pool_scale), j)
        else:
            x = _conv_mixer(x, norm_rows, 3 * i + 1, w_pw1, row3(conv_b_pw1), conv_w_dw, row3(conv_b_dw),
                            row3(conv_ln_g), row3(conv_ln_b), w_pw2, row3(conv_b_pw2), j)
        x = _ffn(x.reshape(b * s, d), norm_rows, w_in, w_out, i, 1, 3 * i + 2).reshape(b, s, d)
    return _final_norm(x.reshape(b * s, d), final_g.reshape(1, d)).reshape(b, s, d)
```

```python
import functools

import jax
import jax.numpy as jnp
from jax import lax
from jax.experimental import pallas as pl
from jax.experimental.pallas import tpu as pltpu

POOL_WINDOWS = (2, 4, 8, 16)
N_MIXERS = 2
RMS_EPS = 1e-6
LN_EPS = 1e-5
FFN_RESIDUAL_WEIGHT = 0.5

V7X_MXU_DIM = 256
V7X_BF16_SUBLANES = 16
V7X_LANES = 128
V7X_SUBLANES = 8
V7X_VMEM_BYTES = 64 * 2**20

FFN_TOKEN_TILE = 512
FFN_HIDDEN_CHUNK = V7X_MXU_DIM
MIXER_TOKEN_TILE = 512
POOL_HALO = 8
CONV_HALO = V7X_BF16_SUBLANES
SLABS_PER_PLANE = 4
POOL_ROW_CHUNK = 128
CONV_ROW_CHUNK = 64
LN_ROW_CHUNK = 64

BF16 = jnp.bfloat16
F32 = jnp.float32


def _rmsnorm(x, g):
    return x * lax.rsqrt(jnp.mean(x * x, axis=-1, keepdims=True) + RMS_EPS) * g


def _vmem_limit(resident_bytes):
    return int(min(resident_bytes + 16 * 2**20, V7X_VMEM_BYTES - 8 * 2**20))


def _ffn_kernel(x_ref, g_ref, win_ref, wout_ref, og_ref, o_ref, act_ref, *, d_ff, chunk, out_norm):
    x = x_ref[...]
    h = _rmsnorm(x, g_ref[...]).astype(BF16)
    for c in range(d_ff // chunk):
        lo = c * chunk
        gate = jnp.dot(h, win_ref[:, lo:lo + chunk], preferred_element_type=F32)
        up = jnp.dot(h, win_ref[:, d_ff + lo:d_ff + lo + chunk], preferred_element_type=F32)
        act_ref[:, lo:lo + chunk] = (gate * jax.nn.sigmoid(gate) * up).astype(BF16)
    y = jnp.dot(act_ref[...], wout_ref[...], preferred_element_type=F32)
    out = x + FFN_RESIDUAL_WEIGHT * y
    o_ref[...] = _rmsnorm(out, og_ref[...]) if out_norm else out


def _ffn(x2d, norm_g, w_in, w_out, out_g, layer, which, norm_row, out_norm):
    n_tok, d = x2d.shape
    d_ff = w_out.shape[2]
    tm = FFN_TOKEN_TILE
    assert n_tok % tm == 0 and d_ff % FFN_HIDDEN_CHUNK == 0
    resident = (2 * 2 * tm * d * 4) + 2 * (w_in[0, 0].size + w_out[0, 0].size) * 2 + tm * d_ff * 2
    return pl.pallas_call(
        functools.partial(_ffn_kernel, d_ff=d_ff, chunk=FFN_HIDDEN_CHUNK, out_norm=out_norm),
        grid=(n_tok // tm,),
        in_specs=[
            pl.BlockSpec((tm, d), lambda t: (t, 0)),
            pl.BlockSpec((None, 1, d), lambda t: (norm_row, 0, 0)),
            pl.BlockSpec((None, None, d, 2 * d_ff), lambda t: (layer, which, 0, 0)),
            pl.BlockSpec((None, None, d_ff, d), lambda t: (layer, which, 0, 0)),
            pl.BlockSpec((1, d), lambda t: (0, 0)),
        ],
        out_specs=pl.BlockSpec((tm, d), lambda t: (t, 0)),
        out_shape=jax.ShapeDtypeStruct((n_tok, d), x2d.dtype),
        scratch_shapes=[pltpu.VMEM((tm, d_ff), BF16)],
        compiler_params=pltpu.CompilerParams(
            dimension_semantics=("arbitrary",), vmem_limit_bytes=_vmem_limit(resident)),
        name=f"ffn_l{layer}_{which}",
    )(x2d, norm_g, w_in, w_out, out_g)


def _pool_kernel(xp_ref, x_ref, xn_ref, g_ref, w_ref, b_ref, s_ref, o_ref, h_ref, pooled_ref, *, seq_len, tile):
    i = pl.program_id(1)
    last = pl.num_programs(1) - 1
    g = g_ref[...]
    halo = POOL_HALO
    lw = V7X_LANES
    spp = SLABS_PER_PLANE
    d = x_ref.shape[-1]
    pieces = ((0, jnp.where(i > 0, _rmsnorm(xp_ref[...], g), 0.0)),
              (halo, _rmsnorm(x_ref[...], g)),
              (halo + tile, jnp.where(i < last, _rmsnorm(xn_ref[...], g), 0.0)))
    for slab in range(d // lw):
        plane, phase = divmod(slab, spp)
        for row0, piece in pieces:
            h_ref[plane, pl.ds(spp * row0 + phase, piece.shape[0], stride=spp), :] = (
                piece[:, slab * lw:(slab + 1) * lw])

    gd = w_ref.shape[-1]
    rc = POOL_ROW_CHUNK
    for r0 in range(0, tile, rc):
        t = i * tile + r0 + lax.broadcasted_iota(jnp.int32, (rc, 1), 0)
        for gi, w in enumerate(POOL_WINDOWS):
            lo = t - w // 2
            cnt = (jnp.minimum(lo + w, seq_len) - jnp.maximum(lo, 0)).astype(F32)
            inv_cnt = 1.0 / cnt
            for slab in range(gi * gd // lw, (gi + 1) * gd // lw):
                plane, phase = divmod(slab, spp)
                rows_at = lambda off: h_ref[plane, pl.ds(spp * (halo + r0 + off) + phase, rc, stride=spp), :]
                win_sum = rows_at(-(w // 2))
                for k in range(1, w):
                    win_sum = win_sum + rows_at(k - w // 2)
                pooled = win_sum * inv_cnt - rows_at(0)
                pooled_ref[r0:r0 + rc, slab * lw:(slab + 1) * lw] = pooled.astype(BF16)

    for gi in range(len(POOL_WINDOWS)):
        lanes = slice(gi * gd, (gi + 1) * gd)
        y = jnp.dot(pooled_ref[:, lanes], w_ref[gi], preferred_element_type=F32) + b_ref[:, lanes]
        o_ref[:, lanes] = x_ref[:, lanes] + y * s_ref[:, lanes]


def _pool_mixer(x, norm_g, norm_row, pool_w, pool_b, pool_scale, j):
    b, s, d = x.shape
    tile = MIXER_TOKEN_TILE
    halo = POOL_HALO
    assert s % tile == 0 and tile % halo == 0 and tile % POOL_ROW_CHUNK == 0
    assert d % (SLABS_PER_PLANE * V7X_LANES) == 0 and pool_w.shape[-1] % V7X_LANES == 0
    assert max(POOL_WINDOWS) // 2 <= halo
    tiles_per_halo = tile // halo
    n_halo_blocks = s // halo
    row_spec = lambda arr_row: pl.BlockSpec((None, 1, d), lambda bi, i: (arr_row, 0, 0))
    return pl.pallas_call(
        functools.partial(_pool_kernel, seq_len=s, tile=tile),
        grid=(b, s // tile),
        in_specs=[
            pl.BlockSpec((None, halo, d), lambda bi, i: (bi, jnp.maximum(i * tiles_per_halo - 1, 0), 0)),
            pl.BlockSpec((None, tile, d), lambda bi, i: (bi, i, 0)),
            pl.BlockSpec((None, halo, d),
                         lambda bi, i: (bi, jnp.minimum((i + 1) * tiles_per_halo, n_halo_blocks - 1), 0)),
            row_spec(norm_row),
            pl.BlockSpec((None,) + pool_w.shape[1:], lambda bi, i: (j, 0, 0, 0)),
            row_spec(j),
            row_spec(j),
        ],
        out_specs=pl.BlockSpec((None, tile, d), lambda bi, i: (bi, i, 0)),
        out_shape=jax.ShapeDtypeStruct(x.shape, x.dtype),
        scratch_shapes=[
            pltpu.VMEM((d // (SLABS_PER_PLANE * V7X_LANES), SLABS_PER_PLANE * (tile + 2 * halo), V7X_LANES), F32),
            pltpu.VMEM((tile, d), BF16),
        ],
        compiler_params=pltpu.CompilerParams(dimension_semantics=("arbitrary", "arbitrary")),
        name=f"pool_mixer_{j}",
    )(x, x, x, norm_g, pool_w, pool_b, pool_scale)


def _conv_kernel(xp_ref, x_ref, xn_ref, g_ref, w1_ref, b1_ref, wdw_ref, bdw_ref, lng_ref, lnb_ref,
                 w2_ref, b2_ref, o_ref, h_ref, glu_ref, conv_ref, act_ref, *, tile, n_taps):
    i = pl.program_id(1)
    last = pl.num_programs(1) - 1
    g = g_ref[...]
    halo = CONV_HALO
    rows = tile + 2 * halo
    d = x_ref.shape[-1]
    lw = V7X_LANES
    h_ref[0:halo, :] = _rmsnorm(xp_ref[...], g).astype(BF16)
    h_ref[halo:halo + tile, :] = _rmsnorm(x_ref[...], g).astype(BF16)
    h_ref[halo + tile:, :] = _rmsnorm(xn_ref[...], g).astype(BF16)

    spp = SLABS_PER_PLANE
    n_planes = d // (spp * lw)
    h = h_ref[...]
    for c in range(d // (2 * lw)):
        lo = 2 * lw * c
        a = jnp.dot(h, w1_ref[:, lo:lo + 2 * lw], preferred_element_type=F32) + b1_ref[:, lo:lo + 2 * lw]
        gate = (jnp.dot(h, w1_ref[:, d + lo:d + lo + 2 * lw], preferred_element_type=F32)
                + b1_ref[:, d + lo:d + lo + 2 * lw])
        glu = a * jax.nn.sigmoid(gate)
        pieces = ((0, jnp.where(i > 0, glu[:halo], 0.0)),
                  (halo, glu[halo:halo + tile]),
                  (halo + tile, jnp.where(i < last, glu[halo + tile:], 0.0)))
        for half in range(2):
            plane, phase = divmod(2 * c + half, spp)
            for row0, piece in pieces:
                glu_ref[plane, pl.ds(spp * row0 + phase, piece.shape[0], stride=spp), :] = (
                    piece[:, half * lw:(half + 1) * lw])

    first = halo - n_taps // 2
    rc = CONV_ROW_CHUNK
    sub = V7X_SUBLANES
    for plane in range(n_planes):
        slab_lanes = [slice((plane * spp + p) * lw, (plane * spp + p + 1) * lw) for p in range(spp)]
        for r0 in range(0, tile, rc):
            acc = [jnp.broadcast_to(bdw_ref[:, slab_lanes[p]], (rc // sub, sub, lw)) for p in range(spp)]
            for k in range(n_taps):
                for p in range(spp):
                    window = glu_ref[plane, pl.ds(spp * (r0 + first + k) + p, rc, stride=spp), :]
                    tap = jnp.broadcast_to(wdw_ref[k:k + 1, slab_lanes[p]], (sub, lw))
                    acc[p] = acc[p] + window.reshape(rc // sub, sub, lw) * tap
            for p in range(spp):
                conv_ref[r0:r0 + rc, slab_lanes[p]] = acc[p].reshape(rc, lw)

    lc = LN_ROW_CHUNK
    for r0 in range(0, tile, lc):
        v = conv_ref[r0:r0 + lc, :]
        mu = jnp.mean(v, axis=-1, keepdims=True)
        xc = v - mu
        var = jnp.mean(xc * xc, axis=-1, keepdims=True)
        v = xc * lax.rsqrt(var + LN_EPS) * lng_ref[...] + lnb_ref[...]
        act_ref[r0:r0 + lc, :] = (v * jax.nn.sigmoid(v)).astype(BF16)
    y = jnp.dot(act_ref[...], w2_ref[...], preferred_element_type=F32) + b2_ref[...]
    o_ref[...] = x_ref[...] + y


def _conv_mixer(x, norm_g, norm_row, w_pw1, b_pw1, w_dw, b_dw, ln_g, ln_b, w_pw2, b_pw2, j):
    b, s, d = x.shape
    tile = MIXER_TOKEN_TILE
    halo = CONV_HALO
    n_taps = w_dw.shape[1]
    assert s % tile == 0 and tile % halo == 0 and n_taps // 2 <= halo
    assert tile % CONV_ROW_CHUNK == 0 and tile % LN_ROW_CHUNK == 0
    assert d % (SLABS_PER_PLANE * V7X_LANES) == 0 and SLABS_PER_PLANE % 2 == 0
    tiles_per_halo = tile // halo
    n_halo_blocks = s // halo
    row_spec = lambda arr_row, width: pl.BlockSpec((None, 1, width), lambda bi, i: (arr_row, 0, 0))
    resident = 2 * 2 * (w_pw1[0].size + w_pw2[0].size) + 4 * 2 * tile * d * 4 + 4 * (tile + 2 * halo) * d * 4
    return pl.pallas_call(
        functools.partial(_conv_kernel, tile=tile, n_taps=n_taps),
        grid=(b, s // tile),
        in_specs=[
            pl.BlockSpec((None, halo, d), lambda bi, i: (bi, jnp.maximum(i * tiles_per_halo - 1, 0), 0)),
            pl.BlockSpec((None, tile, d), lambda bi, i: (bi, i, 0)),
            pl.BlockSpec((None, halo, d),
                         lambda bi, i: (bi, jnp.minimum((i + 1) * tiles_per_halo, n_halo_blocks - 1), 0)),
            row_spec(norm_row, d),
            pl.BlockSpec((None, d, 2 * d), lambda bi, i: (j, 0, 0)),
            row_spec(j, 2 * d),
            pl.BlockSpec((None, n_taps, d), lambda bi, i: (j, 0, 0)),
            row_spec(j, d),
            row_spec(j, d),
            row_spec(j, d),
            pl.BlockSpec((None, d, d), lambda bi, i: (j, 0, 0)),
            row_spec(j, d),
        ],
        out_specs=pl.BlockSpec((None, tile, d), lambda bi, i: (bi, i, 0)),
        out_shape=jax.ShapeDtypeStruct(x.shape, x.dtype),
        scratch_shapes=[
            pltpu.VMEM((tile + 2 * halo, d), BF16),
            pltpu.VMEM((d // (SLABS_PER_PLANE * V7X_LANES), SLABS_PER_PLANE * (tile + 2 * halo),
                        V7X_LANES), F32),
            pltpu.VMEM((tile, d), F32),
            pltpu.VMEM((tile, d), BF16),
        ],
        compiler_params=pltpu.CompilerParams(
            dimension_semantics=("arbitrary", "arbitrary"), vmem_limit_bytes=_vmem_limit(resident)),
        name=f"conv_mixer_{j}",
    )(x, x, x, norm_g, w_pw1, b_pw1, w_dw, b_dw, ln_g, ln_b, w_pw2, b_pw2)


def kernel(x, norm_g, ffn_w_in, ffn_w_out, pool_w, pool_b, pool_scale, conv_w_pw1, conv_b_pw1, conv_w_dw,
           conv_b_dw, conv_ln_g, conv_ln_b, conv_w_pw2, conv_b_pw2, final_g):
    b, s, d = x.shape
    depth = norm_g.shape[0]
    row3 = lambda a: a.reshape(a.shape[0], 1, a.shape[1])
    norm_rows = norm_g.reshape(depth * 3, 1, d)
    w_in = ffn_w_in.astype(BF16)
    w_out = ffn_w_out.astype(BF16)
    pool_w16 = pool_w.astype(BF16)
    w_pw1 = conv_w_pw1.astype(BF16)
    w_pw2 = conv_w_pw2.astype(BF16)

    out_g = final_g.reshape(1, d)
    assert depth >= 1

    for i in range(depth):
        x = _ffn(x.reshape(b * s, d), norm_rows, w_in, w_out, out_g, i, 0, 3 * i, False).reshape(b, s, d)
        j = i // N_MIXERS
        if i % N_MIXERS == 0:
            x = _pool_mixer(x, norm_rows, 3 * i + 1, pool_w16, row3(pool_b), row3(pool_scale), j)
        else:
            x = _conv_mixer(x, norm_rows, 3 * i + 1, w_pw1, row3(conv_b_pw1), conv_w_dw, row3(conv_b_dw),
                            row3(conv_ln_g), row3(conv_ln_b), w_pw2, row3(conv_b_pw2), j)
        x = _ffn(x.reshape(b * s, d), norm_rows, w_in, w_out, out_g, i, 1, 3 * i + 2,
                 i == depth - 1).reshape(b, s, d)
    return x
```

```python
import functools

import jax
import jax.numpy as jnp
from jax import lax
from jax.experimental import pallas as pl
from jax.experimental.pallas import tpu as pltpu

POOL_WINDOWS = (2, 4, 8, 16)
N_MIXERS = 2
RMS_EPS = 1e-6
LN_EPS = 1e-5
FFN_RESIDUAL_WEIGHT = 0.5

V7X_MXU_DIM = 256
V7X_BF16_SUBLANES = 16
V7X_LANES = 128
V7X_SUBLANES = 8
V7X_VMEM_BYTES = 64 * 2**20

TOKEN_TILE = 512
FFN_HIDDEN_CHUNK = V7X_MXU_DIM
SIDE_WORK_LAG = 4
POOL_HALO = 8
CONV_HALO = V7X_BF16_SUBLANES
SLABS_PER_PLANE = 4
POOL_ROW_CHUNK = 128
CONV_ROW_CHUNK = 64
LN_ROW_CHUNK = 64

BF16 = jnp.bfloat16
F32 = jnp.float32


def _rmsnorm(x, g):
    return x * lax.rsqrt(jnp.mean(x * x, axis=-1, keepdims=True) + RMS_EPS) * g


def _vmem_limit(resident_bytes):
    return int(min(resident_bytes + 24 * 2**20, V7X_VMEM_BYTES - 4 * 2**20))


def _const_index(*idx):
    return lambda *grid_indices: idx


def _slab_plane_shape(rows, d):
    return (d // (SLABS_PER_PLANE * V7X_LANES), SLABS_PER_PLANE * rows, V7X_LANES)


def _store_slabs(dst_ref, first_slab, pieces):
    lw = V7X_LANES
    for row0, piece in pieces:
        for s in range(piece.shape[1] // lw):
            plane, phase = divmod(first_slab + s, SLABS_PER_PLANE)
            dst_ref[plane, pl.ds(SLABS_PER_PLANE * row0 + phase, piece.shape[0], stride=SLABS_PER_PLANE), :] = (
                piece[:, s * lw:(s + 1) * lw])


def _load_slab_rows(src_ref, slab, row0, n_rows):
    plane, phase = divmod(slab, SLABS_PER_PLANE)
    return src_ref[plane, pl.ds(SLABS_PER_PLANE * row0 + phase, n_rows, stride=SLABS_PER_PLANE), :]


def _halo_index_maps(tile, halo, seq_len, tile_of_step):
    blocks_per_tile = tile // halo
    last_block = seq_len // halo - 1

    def prev_halo(*g):
        bi, i = tile_of_step(*g)
        return bi, jnp.maximum(i * blocks_per_tile - 1, 0), 0

    def cur(*g):
        bi, i = tile_of_step(*g)
        return bi, i, 0

    def next_halo(*g):
        bi, i = tile_of_step(*g)
        return bi, jnp.minimum((i + 1) * blocks_per_tile, last_block), 0

    return prev_halo, cur, next_halo


def _zero_after(v):
    bits = lax.bitcast_convert_type(v, jnp.uint32)
    zero = lax.shift_right_logical(lax.shift_right_logical(bits, jnp.uint32(16)), jnp.uint32(16))
    return lax.bitcast_convert_type(zero, F32)


def _ffn_slots(d, d_ff):
    n_in = 2 * (d_ff // FFN_HIDDEN_CHUNK)
    return n_in, n_in + d // FFN_HIDDEN_CHUNK


def _ffn_body(h, x_res, win_ref, wout_ref, act_ref, side_work=None):
    d_ff, d = wout_ref.shape
    chunk = FFN_HIDDEN_CHUNK
    n_in, _ = _ffn_slots(d, d_ff)
    side_work = side_work or {}

    def run_side_work(slot, result):
        for work in side_work.get(slot, ()):
            work(_zero_after(result[:V7X_SUBLANES, :V7X_LANES]))

    for c in range(d_ff // chunk):
        lo = c * chunk
        gate = jnp.dot(h, win_ref[:, lo:lo + chunk], preferred_element_type=F32)
        run_side_work(2 * c, gate)
        up = jnp.dot(h, win_ref[:, d_ff + lo:d_ff + lo + chunk], preferred_element_type=F32)
        act_ref[:, lo:lo + chunk] = (gate * jax.nn.sigmoid(gate) * up).astype(BF16)
        run_side_work(2 * c + 1, up)
    out = []
    for c in range(d // chunk):
        lo = c * chunk
        y = jnp.dot(act_ref[...], wout_ref[:, lo:lo + chunk], preferred_element_type=F32)
        out.append(x_res[:, lo:lo + chunk] + FFN_RESIDUAL_WEIGHT * y)
        run_side_work(n_in + c, y)
    return jnp.concatenate(out, axis=-1)


def _ffn_kernel(x_ref, g_ref, win_ref, wout_ref, og_ref, o_ref, act_ref, *, out_norm):
    x = x_ref[...]
    h = _rmsnorm(x, g_ref[...]).astype(BF16)
    out = _ffn_body(h, x, win_ref, wout_ref, act_ref)
    o_ref[...] = _rmsnorm(out, og_ref[...]) if out_norm else out


def _ffn(x2d, norm_g, w_in, w_out, out_g, layer, which, out_norm):
    n_tok, d = x2d.shape
    d_ff = w_out.shape[2]
    tm = TOKEN_TILE
    assert n_tok % tm == 0 and d_ff % FFN_HIDDEN_CHUNK == 0
    resident = (2 * 2 * tm * d * 4) + (w_in[0, 0].size + w_out[0, 0].size) * 2 + tm * d_ff * 2
    return pl.pallas_call(
        functools.partial(_ffn_kernel, out_norm=out_norm),
        grid=(n_tok // tm,),
        in_specs=[
            pl.BlockSpec((tm, d), lambda t: (t, 0)),
            pl.BlockSpec((None, 1, d), _const_index(3 * layer + 2 * which, 0, 0)),
            pl.BlockSpec((None, None, d, 2 * d_ff), _const_index(layer, which, 0, 0),
                         pipeline_mode=pl.Buffered(1)),
            pl.BlockSpec((None, None, d_ff, d), _const_index(layer, which, 0, 0),
                         pipeline_mode=pl.Buffered(1)),
            pl.BlockSpec((1, d), _const_index(0, 0)),
        ],
        out_specs=pl.BlockSpec((tm, d), lambda t: (t, 0)),
        out_shape=jax.ShapeDtypeStruct((n_tok, d), x2d.dtype),
        scratch_shapes=[pltpu.VMEM((tm, d_ff), BF16)],
        compiler_params=pltpu.CompilerParams(
            dimension_semantics=("arbitrary",), vmem_limit_bytes=_vmem_limit(resident)),
        name=f"ffn_l{layer}_{which}",
    )(x2d, norm_g, w_in, w_out, out_g)


def _pool_ffn_kernel(xp_ref, x_ref, xn_ref, gm_ref, w_ref, b_ref, s_ref, gf_ref, win_ref, wout_ref, og_ref,
                     o_ref, xmid_ref, hmid_ref, xres_ref, act_ref, h_ref, pooled_ref,
                     *, seq_len, tiles_per_seq, out_norm):
    tile, d = x_ref.shape
    halo = POOL_HALO
    lw = V7X_LANES
    gd = w_ref.shape[-1]
    rc = POOL_ROW_CHUNK
    t = pl.program_id(0)
    n_tiles = pl.num_programs(0) - 1
    i = lax.rem(jnp.minimum(t, n_tiles - 1), tiles_per_seq)

    @pl.when(t == 0)
    def _():
        xmid_ref[...] = jnp.zeros(xmid_ref.shape, F32)
        hmid_ref[...] = jnp.zeros(hmid_ref.shape, BF16)

    def norm_to_slabs(sync):
        del sync
        g = gm_ref[...]
        _store_slabs(h_ref, 0, ((0, jnp.where(i == 0, 0.0, _rmsnorm(xp_ref[...], g))),
                                (halo, _rmsnorm(x_ref[...], g)),
                                (halo + tile, jnp.where(i == tiles_per_seq - 1, 0.0, _rmsnorm(xn_ref[...], g)))))

    def pool_rows(r0, sync):
        sync_rows = jnp.tile(sync, (rc // V7X_SUBLANES, 1))
        pos = i * tile + r0 + lax.broadcasted_iota(jnp.int32, (rc, 1), 0)
        for gi, w in enumerate(POOL_WINDOWS):
            lo = pos - w // 2
            cnt = (jnp.minimum(lo + w, seq_len) - jnp.maximum(lo, 0)).astype(F32)
            inv_cnt = 1.0 / cnt
            for slab in range(gi * gd // lw, (gi + 1) * gd // lw):
                win_sum = _load_slab_rows(h_ref, slab, halo + r0 - w // 2, rc) + sync_rows
                for k in range(1, w):
                    win_sum = win_sum + _load_slab_rows(h_ref, slab, halo + r0 - w // 2 + k, rc)
                pooled = win_sum * inv_cnt - _load_slab_rows(h_ref, slab, halo + r0, rc)
                pooled_ref[r0:r0 + rc, slab * lw:(slab + 1) * lw] = pooled.astype(BF16)

    def mix_rows(r0, sync):
        del sync
        for gi in range(len(POOL_WINDOWS)):
            lanes = slice(gi * gd, (gi + 1) * gd)
            y = (jnp.dot(pooled_ref[r0:r0 + rc, lanes], w_ref[gi], preferred_element_type=F32)
                 + b_ref[:, lanes])
            xmid_ref[r0:r0 + rc, lanes] = x_ref[r0:r0 + rc, lanes] + y * s_ref[:, lanes]

    def norm_mixed(sync):
        del sync
        hmid_ref[...] = _rmsnorm(xmid_ref[...], gf_ref[...]).astype(BF16)

    n_in, n_slots = _ffn_slots(d, wout_ref.shape[0])
    row_chunks = list(range(0, tile, rc))
    spacing = (n_in - 2 - SIDE_WORK_LAG) // len(row_chunks)
    assert spacing >= 1
    side_work = {slot: [] for slot in range(n_slots)}
    side_work[0].append(norm_to_slabs)
    for k, r0 in enumerate(row_chunks):
        side_work[1 + k * spacing].append(functools.partial(pool_rows, r0))
        side_work[1 + k * spacing + SIDE_WORK_LAG].append(functools.partial(mix_rows, r0))
    side_work[n_in - 1].append(norm_mixed)

    xres_ref[...] = xmid_ref[...]
    out = _ffn_body(hmid_ref[...], xres_ref[...], win_ref, wout_ref, act_ref, side_work)
    o_ref[...] = _rmsnorm(out, og_ref[...]) if out_norm else out


def _pool_ffn(x, norm_g, pool_w, pool_b, pool_scale, w_in, w_out, out_g, layer, j, out_norm):
    b, s, d = x.shape
    d_ff = w_out.shape[2]
    tile = TOKEN_TILE
    halo = POOL_HALO
    assert s % tile == 0 and tile % halo == 0 and tile % POOL_ROW_CHUNK == 0 and d_ff % FFN_HIDDEN_CHUNK == 0
    assert d % (SLABS_PER_PLANE * V7X_LANES) == 0 and pool_w.shape[-1] % V7X_LANES == 0
    assert max(POOL_WINDOWS) // 2 <= halo
    tiles_per_seq = s // tile
    n_tiles = b * tiles_per_seq

    def mixer_tile(t):
        tb = jnp.minimum(t, n_tiles - 1)
        return tb // tiles_per_seq, tb % tiles_per_seq

    def ffn_tile(t):
        ta = jnp.maximum(t - 1, 0)
        return ta // tiles_per_seq, ta % tiles_per_seq, 0

    prev_halo, cur, next_halo = _halo_index_maps(tile, halo, s, mixer_tile)
    resident = ((w_in[0, 0].size + w_out[0, 0].size) * 2 + 2 * pool_w[0].size * 2 + 2 * 2 * tile * d * 4
                + 2 * tile * d * 4 + 2 * tile * d * 2 + tile * d_ff * 2 + (tile + 2 * halo) * d * 4)
    return pl.pallas_call(
        functools.partial(_pool_ffn_kernel, seq_len=s, tiles_per_seq=tiles_per_seq, out_norm=out_norm),
        grid=(n_tiles + 1,),
        in_specs=[
            pl.BlockSpec((None, halo, d), prev_halo),
            pl.BlockSpec((None, tile, d), cur),
            pl.BlockSpec((None, halo, d), next_halo),
            pl.BlockSpec((None, 1, d), _const_index(3 * layer + 1, 0, 0)),
            pl.BlockSpec((None,) + pool_w.shape[1:], _const_index(j, 0, 0, 0)),
            pl.BlockSpec((None, 1, d), _const_index(j, 0, 0)),
            pl.BlockSpec((None, 1, d), _const_index(j, 0, 0)),
            pl.BlockSpec((None, 1, d), _const_index(3 * layer + 2, 0, 0)),
            pl.BlockSpec((None, None, d, 2 * d_ff), _const_index(layer, 1, 0, 0), pipeline_mode=pl.Buffered(1)),
            pl.BlockSpec((None, None, d_ff, d), _const_index(layer, 1, 0, 0), pipeline_mode=pl.Buffered(1)),
            pl.BlockSpec((1, d), _const_index(0, 0)),
        ],
        out_specs=pl.BlockSpec((None, tile, d), ffn_tile),
        out_shape=jax.ShapeDtypeStruct(x.shape, x.dtype),
        scratch_shapes=[
            pltpu.VMEM((tile, d), F32),
            pltpu.VMEM((tile, d), BF16),
            pltpu.VMEM((tile, d), F32),
            pltpu.VMEM((tile, d_ff), BF16),
            pltpu.VMEM(_slab_plane_shape(tile + 2 * halo, d), F32),
            pltpu.VMEM((tile, d), BF16),
        ],
        compiler_params=pltpu.CompilerParams(
            dimension_semantics=("arbitrary",), vmem_limit_bytes=_vmem_limit(resident)),
        name=f"pool_ffn_l{layer}",
    )(x, x, x, norm_g, pool_w, pool_b, pool_scale, norm_g, w_in, w_out, out_g)


def _conv_kernel(xp_ref, x_ref, xn_ref, g_ref, w1_ref, b1_ref, wdw_ref, bdw_ref, lng_ref, lnb_ref,
                 w2_ref, b2_ref, o_ref, h_ref, glu_ref, conv_ref, act_ref):
    tile, d = x_ref.shape
    n_taps = wdw_ref.shape[0]
    i = pl.program_id(1)
    last = pl.num_programs(1) - 1
    g = g_ref[...]
    halo = CONV_HALO
    lw = V7X_LANES
    spp = SLABS_PER_PLANE
    h_ref[0:halo, :] = _rmsnorm(xp_ref[...], g).astype(BF16)
    h_ref[halo:halo + tile, :] = _rmsnorm(x_ref[...], g).astype(BF16)
    h_ref[halo + tile:, :] = _rmsnorm(xn_ref[...], g).astype(BF16)

    h = h_ref[...]
    for c in range(d // (2 * lw)):
        lo = 2 * lw * c
        a = jnp.dot(h, w1_ref[:, lo:lo + 2 * lw], preferred_element_type=F32) + b1_ref[:, lo:lo + 2 * lw]
        gate = (jnp.dot(h, w1_ref[:, d + lo:d + lo + 2 * lw], preferred_element_type=F32)
                + b1_ref[:, d + lo:d + lo + 2 * lw])
        glu = a * jax.nn.sigmoid(gate)
        _store_slabs(glu_ref, 2 * c, ((0, jnp.where(i == 0, 0.0, glu[:halo])),
                                      (halo, glu[halo:halo + tile]),
                                      (halo + tile, jnp.where(i == last, 0.0, glu[halo + tile:]))))

    first = halo - n_taps // 2
    rc = CONV_ROW_CHUNK
    sub = V7X_SUBLANES
    for plane in range(d // (spp * lw)):
        slab_lanes = [slice((plane * spp + p) * lw, (plane * spp + p + 1) * lw) for p in range(spp)]
        for r0 in range(0, tile, rc):
            acc = [jnp.broadcast_to(bdw_ref[:, slab_lanes[p]], (rc // sub, sub, lw)) for p in range(spp)]
            for k in range(n_taps):
                for p in range(spp):
                    window = _load_slab_rows(glu_ref, plane * spp + p, r0 + first + k, rc)
                    tap = jnp.broadcast_to(wdw_ref[k:k + 1, slab_lanes[p]], (sub, lw))
                    acc[p] = acc[p] + window.reshape(rc // sub, sub, lw) * tap
            for p in range(spp):
                conv_ref[r0:r0 + rc, slab_lanes[p]] = acc[p].reshape(rc, lw)

    lc = LN_ROW_CHUNK
    for r0 in range(0, tile, lc):
        v = conv_ref[r0:r0 + lc, :]
        mu = jnp.mean(v, axis=-1, keepdims=True)
        xc = v - mu
        var = jnp.mean(xc * xc, axis=-1, keepdims=True)
        v = xc * lax.rsqrt(var + LN_EPS) * lng_ref[...] + lnb_ref[...]
        act_ref[r0:r0 + lc, :] = (v * jax.nn.sigmoid(v)).astype(BF16)
    y = jnp.dot(act_ref[...], w2_ref[...], preferred_element_type=F32) + b2_ref[...]
    o_ref[...] = x_ref[...] + y


def _conv_mixer(x, norm_g, w_pw1, b_pw1, w_dw, b_dw, ln_g, ln_b, w_pw2, b_pw2, layer, j):
    b, s, d = x.shape
    tile = TOKEN_TILE
    halo = CONV_HALO
    n_taps = w_dw.shape[1]
    assert s % tile == 0 and tile % halo == 0 and n_taps // 2 <= halo
    assert tile % CONV_ROW_CHUNK == 0 and tile % LN_ROW_CHUNK == 0
    assert d % (SLABS_PER_PLANE * V7X_LANES) == 0 and SLABS_PER_PLANE % 2 == 0
    prev_halo, cur, next_halo = _halo_index_maps(tile, halo, s, lambda bi, i: (bi, i))
    row_spec = lambda width: pl.BlockSpec((None, 1, width), _const_index(j, 0, 0))
    resident = ((w_pw1[0].size + w_pw2[0].size) * 2 + 2 * 2 * tile * d * 4
                + (tile + 2 * halo) * d * (2 + 4) + tile * d * (4 + 2))
    return pl.pallas_call(
        _conv_kernel,
        grid=(b, s // tile),
        in_specs=[
            pl.BlockSpec((None, halo, d), prev_halo),
            pl.BlockSpec((None, tile, d), cur),
            pl.BlockSpec((None, halo, d), next_halo),
            pl.BlockSpec((None, 1, d), _const_index(3 * layer + 1, 0, 0)),
            pl.BlockSpec((None, d, 2 * d), _const_index(j, 0, 0), pipeline_mode=pl.Buffered(1)),
            row_spec(2 * d),
            pl.BlockSpec((None, n_taps, d), _const_index(j, 0, 0)),
            row_spec(d),
            row_spec(d),
            row_spec(d),
            pl.BlockSpec((None, d, d), _const_index(j, 0, 0), pipeline_mode=pl.Buffered(1)),
            row_spec(d),
        ],
        out_specs=pl.BlockSpec((None, tile, d), cur),
        out_shape=jax.ShapeDtypeStruct(x.shape, x.dtype),
        scratch_shapes=[
            pltpu.VMEM((tile + 2 * halo, d), BF16),
            pltpu.VMEM(_slab_plane_shape(tile + 2 * halo, d), F32),
            pltpu.VMEM((tile, d), F32),
            pltpu.VMEM((tile, d), BF16),
        ],
        compiler_params=pltpu.CompilerParams(
            dimension_semantics=("arbitrary", "arbitrary"), vmem_limit_bytes=_vmem_limit(resident)),
        name=f"conv_mixer_l{layer}",
    )(x, x, x, norm_g, w_pw1, b_pw1, w_dw, b_dw, ln_g, ln_b, w_pw2, b_pw2)


def kernel(x, norm_g, ffn_w_in, ffn_w_out, pool_w, pool_b, pool_scale, conv_w_pw1, conv_b_pw1, conv_w_dw,
           conv_b_dw, conv_ln_g, conv_ln_b, conv_w_pw2, conv_b_pw2, final_g):
    b, s, d = x.shape
    depth = norm_g.shape[0]
    row3 = lambda a: a.reshape(a.shape[0], 1, a.shape[1])
    norm_rows = norm_g.reshape(depth * 3, 1, d)
    w_in = ffn_w_in.astype(BF16)
    w_out = ffn_w_out.astype(BF16)
    pool_w16 = pool_w.astype(BF16)
    w_pw1 = conv_w_pw1.astype(BF16)
    w_pw2 = conv_w_pw2.astype(BF16)
    out_g = final_g.reshape(1, d)
    assert depth >= 1

    def ffn(x, layer, which, out_norm):
        return _ffn(x.reshape(b * s, d), norm_rows, w_in, w_out, out_g, layer, which, out_norm).reshape(b, s, d)

    for i in range(depth):
        x = ffn(x, i, 0, False)
        j = i // N_MIXERS
        last_layer = i == depth - 1
        if i % N_MIXERS == 0:
            x = _pool_ffn(x, norm_rows, pool_w16, row3(pool_b), row3(pool_scale), w_in, w_out, out_g, i, j,
                          last_layer)
        else:
            x = _conv_mixer(x, norm_rows, w_pw1, row3(conv_b_pw1), conv_w_dw, row3(conv_b_dw),
                            row3(conv_ln_g), row3(conv_ln_b), w_pw2, row3(conv_b_pw2), i, j)
            x = ffn(x, i, 1, last_layer)
    return x
```

```python
import functools

import jax
import jax.numpy as jnp
from jax import lax
from jax.experimental import pallas as pl
from jax.experimental.pallas import tpu as pltpu

POOL_WINDOWS = (2, 4, 8, 16)
N_MIXERS = 2
RMS_EPS = 1e-6
LN_EPS = 1e-5
FFN_RESIDUAL_WEIGHT = 0.5

V7X_MXU_DIM = 256
V7X_BF16_SUBLANES = 16
V7X_LANES = 128
V7X_SUBLANES = 8
V7X_VMEM_BYTES = 64 * 2**20

FFN_TOKEN_TILE = 1024
CONV_TOKEN_TILE = 1024
POOL_FFN_TOKEN_TILE = 512
FFN_HIDDEN_CHUNK = V7X_MXU_DIM
NORM_NEXT_OUT_CHUNKS = 2
SIDE_WORK_LAG = 4
POOL_HALO = 8
CONV_HALO = V7X_BF16_SUBLANES
SLABS_PER_PLANE = 4
POOL_ROW_CHUNK = 128
CONV_ROW_CHUNK = 64
LN_ROW_CHUNK = 64

BF16 = jnp.bfloat16
F32 = jnp.float32


def _rmsnorm(x, g):
    return x * lax.rsqrt(jnp.mean(x * x, axis=-1, keepdims=True) + RMS_EPS) * g


def _vmem_limit(resident_bytes):
    return int(min(resident_bytes + 24 * 2**20, V7X_VMEM_BYTES - 4 * 2**20))


def _const_index(*idx):
    return lambda *grid_indices: idx


def _slab_plane_shape(rows, d):
    return (d // (SLABS_PER_PLANE * V7X_LANES), SLABS_PER_PLANE * rows, V7X_LANES)


def _store_slabs(dst_ref, first_slab, pieces):
    lw = V7X_LANES
    for row0, piece in pieces:
        for s in range(piece.shape[1] // lw):
            plane, phase = divmod(first_slab + s, SLABS_PER_PLANE)
            dst_ref[plane, pl.ds(SLABS_PER_PLANE * row0 + phase, piece.shape[0], stride=SLABS_PER_PLANE), :] = (
                piece[:, s * lw:(s + 1) * lw])


def _load_slab_rows(src_ref, slab, row0, n_rows):
    plane, phase = divmod(slab, SLABS_PER_PLANE)
    return src_ref[plane, pl.ds(SLABS_PER_PLANE * row0 + phase, n_rows, stride=SLABS_PER_PLANE), :]


def _halo_index_maps(tile, halo, seq_len, tile_of_step):
    blocks_per_tile = tile // halo
    last_block = seq_len // halo - 1

    def prev_halo(*g):
        bi, i = tile_of_step(*g)
        return bi, jnp.maximum(i * blocks_per_tile - 1, 0), 0

    def cur(*g):
        bi, i = tile_of_step(*g)
        return bi, i, 0

    def next_halo(*g):
        bi, i = tile_of_step(*g)
        return bi, jnp.minimum((i + 1) * blocks_per_tile, last_block), 0

    return prev_halo, cur, next_halo


def _zero_after(v):
    bits = lax.bitcast_convert_type(v, jnp.uint32)
    zero = lax.shift_right_logical(lax.shift_right_logical(bits, jnp.uint32(16)), jnp.uint32(16))
    return lax.bitcast_convert_type(zero, F32)


def _ffn_slots(d, d_ff):
    n_in = 2 * (d_ff // FFN_HIDDEN_CHUNK)
    return n_in, n_in + d // FFN_HIDDEN_CHUNK


def _ffn_body(h_ref, x_res, win_ref, wout_ref, act_ref, side_work=None):
    d_ff, d = wout_ref.shape
    chunk = FFN_HIDDEN_CHUNK
    n_in, _ = _ffn_slots(d, d_ff)
    side_work = side_work or {}

    def run_side_work(slot, result):
        for work in side_work.get(slot, ()):
            work(_zero_after(result[:V7X_SUBLANES, :V7X_LANES]))

    for c in range(d_ff // chunk):
        lo = c * chunk
        gate = jnp.dot(h_ref[...], win_ref[:, lo:lo + chunk], preferred_element_type=F32)
        run_side_work(2 * c, gate)
        up = jnp.dot(h_ref[...], win_ref[:, d_ff + lo:d_ff + lo + chunk], preferred_element_type=F32)
        act_ref[:, lo:lo + chunk] = (gate * jax.nn.sigmoid(gate) * up).astype(BF16)
        run_side_work(2 * c + 1, up)
    out = []
    for c in range(d // chunk):
        lo = c * chunk
        y = jnp.dot(act_ref[...], wout_ref[:, lo:lo + chunk], preferred_element_type=F32)
        out.append(x_res[:, lo:lo + chunk] + FFN_RESIDUAL_WEIGHT * y)
        run_side_work(n_in + c, y)
    return jnp.concatenate(out, axis=-1)


def _ffn_kernel(x_ref, xnext_ref, g_ref, win_ref, wout_ref, og_ref, o_ref, h_ref, act_ref, *, out_norm):
    n_in, _ = _ffn_slots(*wout_ref.shape[::-1])

    @pl.when(pl.program_id(0) == 0)
    def _():
        h_ref[...] = _rmsnorm(x_ref[...], g_ref[...]).astype(BF16)

    h_next = []

    def norm_next(sync):
        x = xnext_ref[...]
        mean_sq = jnp.mean(x * x, axis=-1, keepdims=True) + jnp.tile(sync[:, :1], (x.shape[0] // V7X_SUBLANES, 1))
        h_next.append((x * lax.rsqrt(mean_sq + RMS_EPS) * g_ref[...]).astype(BF16))
        h_ref[...] = h_next[0]

    def norm_next_deadline(sync):
        del sync
        rows, lanes = V7X_BF16_SUBLANES, V7X_LANES
        h = h_next[0]
        every = functools.reduce(lambda a, b: a + b, [h[r:r + rows] for r in range(0, h.shape[0], rows)])
        every = functools.reduce(lambda a, b: a + b, [every[:, c:c + lanes] for c in range(0, h.shape[1], lanes)])
        done = _zero_after(every.astype(F32)).astype(BF16)
        act_ref[0:rows, 0:lanes] = act_ref[0:rows, 0:lanes] + done

    side_work = {n_in - 1: [norm_next], n_in + NORM_NEXT_OUT_CHUNKS - 1: [norm_next_deadline]}
    out = _ffn_body(h_ref, x_ref[...], win_ref, wout_ref, act_ref, side_work)
    o_ref[...] = _rmsnorm(out, og_ref[...]) if out_norm else out


def _ffn(x2d, norm_g, w_in, w_out, out_g, layer, which, out_norm):
    n_tok, d = x2d.shape
    d_ff = w_out.shape[2]
    tm = FFN_TOKEN_TILE
    assert n_tok % tm == 0 and d_ff % FFN_HIDDEN_CHUNK == 0
    n_steps = n_tok // tm
    resident = (3 * 2 * tm * d * 4) + (w_in[0, 0].size + w_out[0, 0].size) * 2 + tm * d_ff * 2 + tm * d * 2
    return pl.pallas_call(
        functools.partial(_ffn_kernel, out_norm=out_norm),
        grid=(n_steps,),
        in_specs=[
            pl.BlockSpec((tm, d), lambda t: (t, 0)),
            pl.BlockSpec((tm, d), lambda t: (jnp.minimum(t + 1, n_steps - 1), 0)),
            pl.BlockSpec((None, 1, d), _const_index(3 * layer + 2 * which, 0, 0)),
            pl.BlockSpec((None, None, d, 2 * d_ff), _const_index(layer, which, 0, 0),
                         pipeline_mode=pl.Buffered(1)),
            pl.BlockSpec((None, None, d_ff, d), _const_index(layer, which, 0, 0),
                         pipeline_mode=pl.Buffered(1)),
            pl.BlockSpec((1, d), _const_index(0, 0)),
        ],
        out_specs=pl.BlockSpec((tm, d), lambda t: (t, 0)),
        out_shape=jax.ShapeDtypeStruct((n_tok, d), x2d.dtype),
        scratch_shapes=[pltpu.VMEM((tm, d), BF16), pltpu.VMEM((tm, d_ff), BF16)],
        compiler_params=pltpu.CompilerParams(
            dimension_semantics=("arbitrary",), vmem_limit_bytes=_vmem_limit(resident)),
        name=f"ffn_l{layer}_{which}",
    )(x2d, x2d, norm_g, w_in, w_out, out_g)


def _pool_ffn_kernel(xp_ref, x_ref, xn_ref, gm_ref, w_ref, b_ref, s_ref, gf_ref, win_ref, wout_ref, og_ref,
                     o_ref, xmid_ref, hmid_ref, xres_ref, act_ref, h_ref, pooled_ref,
                     *, seq_len, tiles_per_seq, out_norm):
    tile, d = x_ref.shape
    halo = POOL_HALO
    lw = V7X_LANES
    gd = w_ref.shape[-1]
    rc = POOL_ROW_CHUNK
    t = pl.program_id(0)
    n_tiles = pl.num_programs(0) - 1
    i = lax.rem(jnp.minimum(t, n_tiles - 1), tiles_per_seq)

    @pl.when(t == 0)
    def _():
        xmid_ref[...] = jnp.zeros(xmid_ref.shape, F32)
        hmid_ref[...] = jnp.zeros(hmid_ref.shape, BF16)

    def norm_to_slabs(sync):
        del sync
        g = gm_ref[...]
        _store_slabs(h_ref, 0, ((0, jnp.where(i == 0, 0.0, _rmsnorm(xp_ref[...], g))),
                                (halo, _rmsnorm(x_ref[...], g)),
                                (halo + tile, jnp.where(i == tiles_per_seq - 1, 0.0, _rmsnorm(xn_ref[...], g)))))

    def pool_rows(r0, sync):
        sync_rows = jnp.tile(sync, (rc // V7X_SUBLANES, 1))
        pos = i * tile + r0 + lax.broadcasted_iota(jnp.int32, (rc, 1), 0)
        for gi, w in enumerate(POOL_WINDOWS):
            lo = pos - w // 2
            cnt = (jnp.minimum(lo + w, seq_len) - jnp.maximum(lo, 0)).astype(F32)
            inv_cnt = 1.0 / cnt
            for slab in range(gi * gd // lw, (gi + 1) * gd // lw):
                win_sum = _load_slab_rows(h_ref, slab, halo + r0 - w // 2, rc) + sync_rows
                for k in range(1, w):
                    win_sum = win_sum + _load_slab_rows(h_ref, slab, halo + r0 - w // 2 + k, rc)
                pooled = win_sum * inv_cnt - _load_slab_rows(h_ref, slab, halo + r0, rc)
                pooled_ref[r0:r0 + rc, slab * lw:(slab + 1) * lw] = pooled.astype(BF16)

    def mix_rows(r0, sync):
        del sync
        for gi in range(len(POOL_WINDOWS)):
            lanes = slice(gi * gd, (gi + 1) * gd)
            y = (jnp.dot(pooled_ref[r0:r0 + rc, lanes], w_ref[gi], preferred_element_type=F32)
                 + b_ref[:, lanes])
            xmid_ref[r0:r0 + rc, lanes] = x_ref[r0:r0 + rc, lanes] + y * s_ref[:, lanes]

    def norm_mixed(sync):
        del sync
        hmid_ref[...] = _rmsnorm(xmid_ref[...], gf_ref[...]).astype(BF16)

    n_in, n_slots = _ffn_slots(d, wout_ref.shape[0])
    row_chunks = list(range(0, tile, rc))
    spacing = (n_in - 2 - SIDE_WORK_LAG) // len(row_chunks)
    assert spacing >= 1
    side_work = {slot: [] for slot in range(n_slots)}
    side_work[0].append(norm_to_slabs)
    for k, r0 in enumerate(row_chunks):
        side_work[1 + k * spacing].append(functools.partial(pool_rows, r0))
        side_work[1 + k * spacing + SIDE_WORK_LAG].append(functools.partial(mix_rows, r0))
    side_work[n_in - 1].append(norm_mixed)

    xres_ref[...] = xmid_ref[...]
    out = _ffn_body(hmid_ref, xres_ref[...], win_ref, wout_ref, act_ref, side_work)
    o_ref[...] = _rmsnorm(out, og_ref[...]) if out_norm else out


def _pool_ffn(x, norm_g, pool_w, pool_b, pool_scale, w_in, w_out, out_g, layer, j, out_norm):
    b, s, d = x.shape
    d_ff = w_out.shape[2]
    tile = POOL_FFN_TOKEN_TILE
    halo = POOL_HALO
    assert s % tile == 0 and tile % halo == 0 and tile % POOL_ROW_CHUNK == 0 and d_ff % FFN_HIDDEN_CHUNK == 0
    assert d % (SLABS_PER_PLANE * V7X_LANES) == 0 and pool_w.shape[-1] % V7X_LANES == 0
    assert max(POOL_WINDOWS) // 2 <= halo
    tiles_per_seq = s // tile
    n_tiles = b * tiles_per_seq

    def mixer_tile(t):
        tb = jnp.minimum(t, n_tiles - 1)
        return tb // tiles_per_seq, tb % tiles_per_seq

    def ffn_tile(t):
        ta = jnp.maximum(t - 1, 0)
        return ta // tiles_per_seq, ta % tiles_per_seq, 0

    prev_halo, cur, next_halo = _halo_index_maps(tile, halo, s, mixer_tile)
    resident = ((w_in[0, 0].size + w_out[0, 0].size) * 2 + 2 * pool_w[0].size * 2 + 2 * 2 * tile * d * 4
                + 2 * tile * d * 4 + 2 * tile * d * 2 + tile * d_ff * 2 + (tile + 2 * halo) * d * 4)
    return pl.pallas_call(
        functools.partial(_pool_ffn_kernel, seq_len=s, tiles_per_seq=tiles_per_seq, out_norm=out_norm),
        grid=(n_tiles + 1,),
        in_specs=[
            pl.BlockSpec((None, halo, d), prev_halo),
            pl.BlockSpec((None, tile, d), cur),
            pl.BlockSpec((None, halo, d), next_halo),
            pl.BlockSpec((None, 1, d), _const_index(3 * layer + 1, 0, 0)),
            pl.BlockSpec((None,) + pool_w.shape[1:], _const_index(j, 0, 0, 0)),
            pl.BlockSpec((None, 1, d), _const_index(j, 0, 0)),
            pl.BlockSpec((None, 1, d), _const_index(j, 0, 0)),
            pl.BlockSpec((None, 1, d), _const_index(3 * layer + 2, 0, 0)),
            pl.BlockSpec((None, None, d, 2 * d_ff), _const_index(layer, 1, 0, 0), pipeline_mode=pl.Buffered(1)),
            pl.BlockSpec((None, None, d_ff, d), _const_index(layer, 1, 0, 0), pipeline_mode=pl.Buffered(1)),
            pl.BlockSpec((1, d), _const_index(0, 0)),
        ],
        out_specs=pl.BlockSpec((None, tile, d), ffn_tile),
        out_shape=jax.ShapeDtypeStruct(x.shape, x.dtype),
        scratch_shapes=[
            pltpu.VMEM((tile, d), F32),
            pltpu.VMEM((tile, d), BF16),
            pltpu.VMEM((tile, d), F32),
            pltpu.VMEM((tile, d_ff), BF16),
            pltpu.VMEM(_slab_plane_shape(tile + 2 * halo, d), F32),
            pltpu.VMEM((tile, d), BF16),
        ],
        compiler_params=pltpu.CompilerParams(
            dimension_semantics=("arbitrary",), vmem_limit_bytes=_vmem_limit(resident)),
        name=f"pool_ffn_l{layer}",
    )(x, x, x, norm_g, pool_w, pool_b, pool_scale, norm_g, w_in, w_out, out_g)


def _conv_kernel(xp_ref, x_ref, xn_ref, g_ref, w1_ref, b1_ref, wdw_ref, bdw_ref, lng_ref, lnb_ref,
                 w2_ref, b2_ref, o_ref, h_ref, glu_ref, conv_ref, act_ref):
    tile, d = x_ref.shape
    n_taps = wdw_ref.shape[0]
    i = pl.program_id(1)
    last = pl.num_programs(1) - 1
    g = g_ref[...]
    halo = CONV_HALO
    lw = V7X_LANES
    spp = SLABS_PER_PLANE
    h_ref[0:halo, :] = _rmsnorm(xp_ref[...], g).astype(BF16)
    h_ref[halo:halo + tile, :] = _rmsnorm(x_ref[...], g).astype(BF16)
    h_ref[halo + tile:, :] = _rmsnorm(xn_ref[...], g).astype(BF16)

    h = h_ref[...]
    for c in range(d // (2 * lw)):
        lo = 2 * lw * c
        a = jnp.dot(h, w1_ref[:, lo:lo + 2 * lw], preferred_element_type=F32) + b1_ref[:, lo:lo + 2 * lw]
        gate = (jnp.dot(h, w1_ref[:, d + lo:d + lo + 2 * lw], preferred_element_type=F32)
                + b1_ref[:, d + lo:d + lo + 2 * lw])
        glu = a * jax.nn.sigmoid(gate)
        _store_slabs(glu_ref, 2 * c, ((0, jnp.where(i == 0, 0.0, glu[:halo])),
                                      (halo, glu[halo:halo + tile]),
                                      (halo + tile, jnp.where(i == last, 0.0, glu[halo + tile:]))))

    first = halo - n_taps // 2
    rc = CONV_ROW_CHUNK
    sub = V7X_SUBLANES
    for plane in range(d // (spp * lw)):
        slab_lanes = [slice((plane * spp + p) * lw, (plane * spp + p + 1) * lw) for p in range(spp)]
        for r0 in range(0, tile, rc):
            acc = [jnp.broadcast_to(bdw_ref[:, slab_lanes[p]], (rc // sub, sub, lw)) for p in range(spp)]
            for k in range(n_taps):
                for p in range(spp):
                    window = _load_slab_rows(glu_ref, plane * spp + p, r0 + first + k, rc)
                    tap = jnp.broadcast_to(wdw_ref[k:k + 1, slab_lanes[p]], (sub, lw))
                    acc[p] = acc[p] + window.reshape(rc // sub, sub, lw) * tap
            for p in range(spp):
                conv_ref[r0:r0 + rc, slab_lanes[p]] = acc[p].reshape(rc, lw)

    lc = LN_ROW_CHUNK
    for r0 in range(0, tile, lc):
        v = conv_ref[r0:r0 + lc, :]
        mu = jnp.mean(v, axis=-1, keepdims=True)
        xc = v - mu
        var = jnp.mean(xc * xc, axis=-1, keepdims=True)
        v = xc * lax.rsqrt(var + LN_EPS) * lng_ref[...] + lnb_ref[...]
        act_ref[r0:r0 + lc, :] = (v * jax.nn.sigmoid(v)).astype(BF16)
    y = jnp.dot(act_ref[...], w2_ref[...], preferred_element_type=F32) + b2_ref[...]
    o_ref[...] = x_ref[...] + y


def _conv_mixer(x, norm_g, w_pw1, b_pw1, w_dw, b_dw, ln_g, ln_b, w_pw2, b_pw2, layer, j):
    b, s, d = x.shape
    tile = CONV_TOKEN_TILE
    halo = CONV_HALO
    n_taps = w_dw.shape[1]
    assert s % tile == 0 and tile % halo == 0 and n_taps // 2 <= halo
    assert tile % CONV_ROW_CHUNK == 0 and tile % LN_ROW_CHUNK == 0
    assert d % (SLABS_PER_PLANE * V7X_LANES) == 0 and SLABS_PER_PLANE % 2 == 0
    prev_halo, cur, next_halo = _halo_index_maps(tile, halo, s, lambda bi, i: (bi, i))
    row_spec = lambda width: pl.BlockSpec((None, 1, width), _const_index(j, 0, 0))
    resident = ((w_pw1[0].size + w_pw2[0].size) * 2 + 2 * 2 * tile * d * 4
                + (tile + 2 * halo) * d * (2 + 4) + tile * d * (4 + 2))
    return pl.pallas_call(
        _conv_kernel,
        grid=(b, s // tile),
        in_specs=[
            pl.BlockSpec((None, halo, d), prev_halo),
            pl.BlockSpec((None, tile, d), cur),
            pl.BlockSpec((None, halo, d), next_halo),
            pl.BlockSpec((None, 1, d), _const_index(3 * layer + 1, 0, 0)),
            pl.BlockSpec((None, d, 2 * d), _const_index(j, 0, 0), pipeline_mode=pl.Buffered(1)),
            row_spec(2 * d),
            pl.BlockSpec((None, n_taps, d), _const_index(j, 0, 0)),
            row_spec(d),
            row_spec(d),
            row_spec(d),
            pl.BlockSpec((None, d, d), _const_index(j, 0, 0), pipeline_mode=pl.Buffered(1)),
            row_spec(d),
        ],
        out_specs=pl.BlockSpec((None, tile, d), cur),
        out_shape=jax.ShapeDtypeStruct(x.shape, x.dtype),
        scratch_shapes=[
            pltpu.VMEM((tile + 2 * halo, d), BF16),
            pltpu.VMEM(_slab_plane_shape(tile + 2 * halo, d), F32),
            pltpu.VMEM((tile, d), F32),
            pltpu.VMEM((tile, d), BF16),
        ],
        compiler_params=pltpu.CompilerParams(
            dimension_semantics=("arbitrary", "arbitrary"), vmem_limit_bytes=_vmem_limit(resident)),
        name=f"conv_mixer_l{layer}",
    )(x, x, x, norm_g, w_pw1, b_pw1, w_dw, b_dw, ln_g, ln_b, w_pw2, b_pw2)


def kernel(x, norm_g, ffn_w_in, ffn_w_out, pool_w, pool_b, pool_scale, conv_w_pw1, conv_b_pw1, conv_w_dw,
           conv_b_dw, conv_ln_g, conv_ln_b, conv_w_pw2, conv_b_pw2, final_g):
    b, s, d = x.shape
    depth = norm_g.shape[0]
    row3 = lambda a: a.reshape(a.shape[0], 1, a.shape[1])
    norm_rows = norm_g.reshape(depth * 3, 1, d)
    w_in = ffn_w_in.astype(BF16)
    w_out = ffn_w_out.astype(BF16)
    pool_w16 = pool_w.astype(BF16)
    w_pw1 = conv_w_pw1.astype(BF16)
    w_pw2 = conv_w_pw2.astype(BF16)
    out_g = final_g.reshape(1, d)
    assert depth >= 1

    def ffn(x, layer, which, out_norm):
        return _ffn(x.reshape(b * s, d), norm_rows, w_in, w_out, out_g, layer, which, out_norm).reshape(b, s, d)

    for i in range(depth):
        x = ffn(x, i, 0, False)
        j = i // N_MIXERS
        last_layer = i == depth - 1
        if i % N_MIXERS == 0:
            x = _pool_ffn(x, norm_rows, pool_w16, row3(pool_b), row3(pool_scale), w_in, w_out, out_g, i, j,
                          last_layer)
        else:
            x = _conv_mixer(x, norm_rows, w_pw1, row3(conv_b_pw1), conv_w_dw, row3(conv_b_dw),
                            row3(conv_ln_g), row3(conv_ln_b), w_pw2, row3(conv_b_pw2), i, j)
            x = ffn(x, i, 1, last_layer)
    return x
```

```python
import functools

import jax
import jax.numpy as jnp
from jax import lax
from jax.experimental import pallas as pl
from jax.experimental.pallas import tpu as pltpu

POOL_WINDOWS = (2, 4, 8, 16)
N_MIXERS = 2
RMS_EPS = 1e-6
LN_EPS = 1e-5
FFN_RESIDUAL_WEIGHT = 0.5

V7X_MXU_DIM = 256
V7X_BF16_SUBLANES = 16
V7X_LANES = 128
V7X_SUBLANES = 8
V7X_VMEM_BYTES = 64 * 2**20

FFN_TOKEN_TILE = 1024
CONV_TOKEN_TILE = 1024
POOL_FFN_TOKEN_TILE = 512
FFN_HIDDEN_CHUNK = V7X_MXU_DIM
NORM_NEXT_OUT_CHUNKS = 2
SIDE_WORK_LAG = 4
POOL_HALO = 8
CONV_HALO = V7X_BF16_SUBLANES
SLABS_PER_PLANE = 4
POOL_ROW_CHUNK = 128
CONV_ROW_CHUNK = 64
LN_ROW_CHUNK = 64

BF16 = jnp.bfloat16
F32 = jnp.float32


def _rmsnorm(x, g):
    return x * lax.rsqrt(jnp.mean(x * x, axis=-1, keepdims=True) + RMS_EPS) * g


def _vmem_limit(resident_bytes):
    return int(min(resident_bytes + 24 * 2**20, V7X_VMEM_BYTES - 4 * 2**20))


def _const_index(*idx):
    return lambda *grid_indices: idx


def _slab_plane_shape(rows, d):
    return (d // (SLABS_PER_PLANE * V7X_LANES), SLABS_PER_PLANE * rows, V7X_LANES)


def _store_slabs(dst_ref, first_slab, pieces):
    lw = V7X_LANES
    for row0, piece in pieces:
        for s in range(piece.shape[1] // lw):
            plane, phase = divmod(first_slab + s, SLABS_PER_PLANE)
            dst_ref[plane, pl.ds(SLABS_PER_PLANE * row0 + phase, piece.shape[0], stride=SLABS_PER_PLANE), :] = (
                piece[:, s * lw:(s + 1) * lw])


def _load_slab_rows(src_ref, slab, row0, n_rows):
    plane, phase = divmod(slab, SLABS_PER_PLANE)
    return src_ref[plane, pl.ds(SLABS_PER_PLANE * row0 + phase, n_rows, stride=SLABS_PER_PLANE), :]


def _halo_index_maps(tile, halo, seq_len, tile_of_step):
    blocks_per_tile = tile // halo
    last_block = seq_len // halo - 1

    def prev_halo(*g):
        bi, i = tile_of_step(*g)
        return bi, jnp.maximum(i * blocks_per_tile - 1, 0), 0

    def cur(*g):
        bi, i = tile_of_step(*g)
        return bi, i, 0

    def next_halo(*g):
        bi, i = tile_of_step(*g)
        return bi, jnp.minimum((i + 1) * blocks_per_tile, last_block), 0

    return prev_halo, cur, next_halo


def _zero_after(v):
    bits = lax.bitcast_convert_type(v, jnp.uint32)
    zero = lax.shift_right_logical(lax.shift_right_logical(bits, jnp.uint32(16)), jnp.uint32(16))
    return lax.bitcast_convert_type(zero, F32)


def _ffn_slots(d, d_ff):
    n_in = 2 * (d_ff // FFN_HIDDEN_CHUNK)
    return n_in, n_in + d // FFN_HIDDEN_CHUNK


def _ffn_body(h, x_res, win_ref, wout_ref, act_ref, side_work=None):
    d_ff, d = wout_ref.shape
    chunk = FFN_HIDDEN_CHUNK
    n_in, _ = _ffn_slots(d, d_ff)
    side_work = side_work or {}

    def run_side_work(slot, result):
        for work in side_work.get(slot, ()):
            work(_zero_after(result[:V7X_SUBLANES, :V7X_LANES]))

    for c in range(d_ff // chunk):
        lo = c * chunk
        gate = jnp.dot(h[...], win_ref[:, lo:lo + chunk], preferred_element_type=F32)
        run_side_work(2 * c, gate)
        up = jnp.dot(h[...], win_ref[:, d_ff + lo:d_ff + lo + chunk], preferred_element_type=F32)
        act_ref[:, lo:lo + chunk] = (gate * jax.nn.sigmoid(gate) * up).astype(BF16)
        run_side_work(2 * c + 1, up)
    out = []
    for c in range(d // chunk):
        lo = c * chunk
        y = jnp.dot(act_ref[...], wout_ref[:, lo:lo + chunk], preferred_element_type=F32)
        out.append(x_res[:, lo:lo + chunk] + FFN_RESIDUAL_WEIGHT * y)
        run_side_work(n_in + c, y)
    return jnp.concatenate(out, axis=-1)


def _cast_ahead_plumbing(next_ffn, n_steps, chunk_step):
    w_in_all, w_out_all, layer, which = next_ffn
    in_specs, out_specs, out_shapes = [], [], []
    for w_all in (w_in_all, w_out_all):
        rows, cols = w_all.shape[2:]
        n_chunks = max(n for n in range(1, n_steps + 1)
                       if n_steps % n == 0 and rows % (n * V7X_BF16_SUBLANES) == 0)
        chunk_rows, steps_per_chunk = rows // n_chunks, n_steps // n_chunks
        in_specs.append(pl.BlockSpec(
            (None, None, chunk_rows, cols),
            lambda t, k=steps_per_chunk: (layer, which, chunk_step(t) // k, 0)))
        out_specs.append(pl.BlockSpec((chunk_rows, cols), lambda t, k=steps_per_chunk: (chunk_step(t) // k, 0)))
        out_shapes.append(jax.ShapeDtypeStruct((rows, cols), BF16))
    return in_specs, out_specs, out_shapes


def _cast_ahead(src_refs, dst_refs, sync):
    for src_ref, dst_ref in zip(src_refs, dst_refs):
        rows, cols = src_ref.shape
        dst_ref[...] = (src_ref[...] + jnp.tile(sync, (rows // V7X_SUBLANES, cols // V7X_LANES))).astype(BF16)


def _ffn_kernel(x_ref, xnext_ref, g_ref, win_ref, wout_ref, og_ref, *rest, out_norm, cast_next):
    n_in, _ = _ffn_slots(*wout_ref.shape[::-1])
    side_work = {slot: [] for slot in range(n_in + NORM_NEXT_OUT_CHUNKS)}
    if cast_next:
        *next_f32_refs, o_ref, next_in_ref, next_out_ref, h_ref, act_ref = rest
        side_work[n_in // 2].append(functools.partial(_cast_ahead, next_f32_refs, (next_in_ref, next_out_ref)))
    else:
        o_ref, h_ref, act_ref = rest

    @pl.when(pl.program_id(0) == 0)
    def _():
        h_ref[...] = _rmsnorm(x_ref[...], g_ref[...]).astype(BF16)

    h_next = []

    def norm_next(sync):
        x = xnext_ref[...]
        mean_sq = jnp.mean(x * x, axis=-1, keepdims=True) + jnp.tile(sync[:, :1], (x.shape[0] // V7X_SUBLANES, 1))
        h_next.append((x * lax.rsqrt(mean_sq + RMS_EPS) * g_ref[...]).astype(BF16))
        h_ref[...] = h_next[0]

    def norm_next_deadline(sync):
        del sync
        rows, lanes = V7X_BF16_SUBLANES, V7X_LANES
        h = h_next[0]
        every = functools.reduce(lambda a, b: a + b, [h[r:r + rows] for r in range(0, h.shape[0], rows)])
        every = functools.reduce(lambda a, b: a + b, [every[:, c:c + lanes] for c in range(0, h.shape[1], lanes)])
        done = _zero_after(every.astype(F32)).astype(BF16)
        act_ref[0:rows, 0:lanes] = act_ref[0:rows, 0:lanes] + done

    side_work[n_in - 1].append(norm_next)
    side_work[n_in + NORM_NEXT_OUT_CHUNKS - 1].append(norm_next_deadline)
    out = _ffn_body(h_ref, x_ref[...], win_ref, wout_ref, act_ref, side_work)
    o_ref[...] = _rmsnorm(out, og_ref[...]) if out_norm else out


def _ffn(x2d, norm_g, w_in, w_out, out_g, layer, which, out_norm, next_ffn):
    n_tok, d = x2d.shape
    d_ff = w_out.shape[0]
    tm = FFN_TOKEN_TILE
    assert n_tok % tm == 0 and d_ff % FFN_HIDDEN_CHUNK == 0
    n_steps = n_tok // tm
    cast_in_specs, cast_out_specs, cast_out_shapes = (
        _cast_ahead_plumbing(next_ffn, n_steps, lambda t: t) if next_ffn else ([], [], []))
    cast_bytes = 3 * (w_in.size + w_out.size) * 2 * 2 // n_steps
    resident = ((3 * 2 * tm * d * 4) + (w_in.size + w_out.size) * 2 + tm * d_ff * 2 + tm * d * 2 + cast_bytes)
    outs = pl.pallas_call(
        functools.partial(_ffn_kernel, out_norm=out_norm, cast_next=bool(next_ffn)),
        grid=(n_steps,),
        in_specs=[
            pl.BlockSpec((tm, d), lambda t: (t, 0)),
            pl.BlockSpec((tm, d), lambda t: (jnp.minimum(t + 1, n_steps - 1), 0)),
            pl.BlockSpec((None, 1, d), _const_index(3 * layer + 2 * which, 0, 0)),
            pl.BlockSpec((d, 2 * d_ff), _const_index(0, 0), pipeline_mode=pl.Buffered(1)),
            pl.BlockSpec((d_ff, d), _const_index(0, 0), pipeline_mode=pl.Buffered(1)),
            pl.BlockSpec((1, d), _const_index(0, 0)),
            *cast_in_specs,
        ],
        out_specs=[pl.BlockSpec((tm, d), lambda t: (t, 0)), *cast_out_specs],
        out_shape=[jax.ShapeDtypeStruct((n_tok, d), x2d.dtype), *cast_out_shapes],
        scratch_shapes=[pltpu.VMEM((tm, d), BF16), pltpu.VMEM((tm, d_ff), BF16)],
        compiler_params=pltpu.CompilerParams(
            dimension_semantics=("arbitrary",), vmem_limit_bytes=_vmem_limit(resident)),
        name=f"ffn_l{layer}_{which}",
    )(x2d, x2d, norm_g, w_in, w_out, out_g, *(next_ffn[:2] if next_ffn else ()))
    return tuple(outs) if next_ffn else (outs[0], None, None)


def _pool_ffn_kernel(xp_ref, x_ref, xn_ref, gm_ref, w_ref, b_ref, s_ref, gf_ref, win_ref, wout_ref, og_ref,
                     *rest, seq_len, tiles_per_seq, out_norm, cast_next):
    cast_work = []
    if cast_next:
        *next_f32_refs, o_ref, next_in_ref, next_out_ref = rest[:5]
        cast_work.append(functools.partial(_cast_ahead, next_f32_refs, (next_in_ref, next_out_ref)))
        rest = rest[5:]
    else:
        o_ref, rest = rest[0], rest[1:]
    xmid_ref, hmid_ref, xres_ref, act_ref, h_ref, pooled_ref = rest
    tile, d = x_ref.shape
    halo = POOL_HALO
    lw = V7X_LANES
    gd = w_ref.shape[-1]
    rc = POOL_ROW_CHUNK
    t = pl.program_id(0)
    n_tiles = pl.num_programs(0) - 1
    i = lax.rem(jnp.minimum(t, n_tiles - 1), tiles_per_seq)

    @pl.when(t == 0)
    def _():
        xmid_ref[...] = jnp.zeros(xmid_ref.shape, F32)
        hmid_ref[...] = jnp.zeros(hmid_ref.shape, BF16)

    def norm_to_slabs(sync):
        del sync
        g = gm_ref[...]
        _store_slabs(h_ref, 0, ((0, jnp.where(i == 0, 0.0, _rmsnorm(xp_ref[...], g))),
                                (halo, _rmsnorm(x_ref[...], g)),
                                (halo + tile, jnp.where(i == tiles_per_seq - 1, 0.0, _rmsnorm(xn_ref[...], g)))))

    def pool_rows(r0, sync):
        sync_rows = jnp.tile(sync, (rc // V7X_SUBLANES, 1))
        pos = i * tile + r0 + lax.broadcasted_iota(jnp.int32, (rc, 1), 0)
        for gi, w in enumerate(POOL_WINDOWS):
            lo = pos - w // 2
            cnt = (jnp.minimum(lo + w, seq_len) - jnp.maximum(lo, 0)).astype(F32)
            inv_cnt = 1.0 / cnt
            for slab in range(gi * gd // lw, (gi + 1) * gd // lw):
                win_sum = _load_slab_rows(h_ref, slab, halo + r0 - w // 2, rc) + sync_rows
                for k in range(1, w):
                    win_sum = win_sum + _load_slab_rows(h_ref, slab, halo + r0 - w // 2 + k, rc)
                pooled = win_sum * inv_cnt - _load_slab_rows(h_ref, slab, halo + r0, rc)
                pooled_ref[r0:r0 + rc, slab * lw:(slab + 1) * lw] = pooled.astype(BF16)

    def mix_rows(r0, sync):
        del sync
        for gi in range(len(POOL_WINDOWS)):
            lanes = slice(gi * gd, (gi + 1) * gd)
            y = (jnp.dot(pooled_ref[r0:r0 + rc, lanes], w_ref[gi], preferred_element_type=F32)
                 + b_ref[:, lanes])
            xmid_ref[r0:r0 + rc, lanes] = x_ref[r0:r0 + rc, lanes] + y * s_ref[:, lanes]

    def norm_mixed(sync):
        del sync
        hmid_ref[...] = _rmsnorm(xmid_ref[...], gf_ref[...]).astype(BF16)

    n_in, n_slots = _ffn_slots(d, wout_ref.shape[0])
    row_chunks = list(range(0, tile, rc))
    spacing = (n_in - 2 - SIDE_WORK_LAG) // len(row_chunks)
    assert spacing >= 1
    side_work = {slot: [] for slot in range(n_slots)}
    side_work[0].append(norm_to_slabs)
    for k, r0 in enumerate(row_chunks):
        side_work[1 + k * spacing].append(functools.partial(pool_rows, r0))
        side_work[1 + k * spacing + SIDE_WORK_LAG].append(functools.partial(mix_rows, r0))
    side_work[n_in - 1].append(norm_mixed)
    side_work[n_in].extend(cast_work)

    xres_ref[...] = xmid_ref[...]
    out = _ffn_body(hmid_ref[...], xres_ref[...], win_ref, wout_ref, act_ref, side_work)
    o_ref[...] = _rmsnorm(out, og_ref[...]) if out_norm else out


def _pool_ffn(x, norm_g, pool_w, pool_b, pool_scale, w_in, w_out, out_g, layer, j, out_norm, next_ffn):
    b, s, d = x.shape
    d_ff = w_out.shape[0]
    tile = POOL_FFN_TOKEN_TILE
    halo = POOL_HALO
    assert s % tile == 0 and tile % halo == 0 and tile % POOL_ROW_CHUNK == 0 and d_ff % FFN_HIDDEN_CHUNK == 0
    assert d % (SLABS_PER_PLANE * V7X_LANES) == 0 and pool_w.shape[-1] % V7X_LANES == 0
    assert max(POOL_WINDOWS) // 2 <= halo
    tiles_per_seq = s // tile
    n_tiles = b * tiles_per_seq

    def mixer_tile(t):
        tb = jnp.minimum(t, n_tiles - 1)
        return tb // tiles_per_seq, tb % tiles_per_seq

    def ffn_tile(t):
        ta = jnp.maximum(t - 1, 0)
        return ta // tiles_per_seq, ta % tiles_per_seq, 0

    prev_halo, cur, next_halo = _halo_index_maps(tile, halo, s, mixer_tile)
    cast_in_specs, cast_out_specs, cast_out_shapes = (
        _cast_ahead_plumbing(next_ffn, n_tiles, lambda t: jnp.minimum(t, n_tiles - 1)) if next_ffn
        else ([], [], []))
    cast_bytes = 3 * (w_in.size + w_out.size) * 2 * 2 // n_tiles
    resident = ((w_in.size + w_out.size) * 2 + 2 * pool_w[0].size * 2 + 2 * 2 * tile * d * 4
                + 2 * tile * d * 4 + 2 * tile * d * 2 + tile * d_ff * 2 + (tile + 2 * halo) * d * 4 + cast_bytes)
    outs = pl.pallas_call(
        functools.partial(_pool_ffn_kernel, seq_len=s, tiles_per_seq=tiles_per_seq, out_norm=out_norm,
                          cast_next=bool(next_ffn)),
        grid=(n_tiles + 1,),
        in_specs=[
            pl.BlockSpec((None, halo, d), prev_halo),
            pl.BlockSpec((None, tile, d), cur),
            pl.BlockSpec((None, halo, d), next_halo),
            pl.BlockSpec((None, 1, d), _const_index(3 * layer + 1, 0, 0)),
            pl.BlockSpec((None,) + pool_w.shape[1:], _const_index(j, 0, 0, 0)),
            pl.BlockSpec((None, 1, d), _const_index(j, 0, 0)),
            pl.BlockSpec((None, 1, d), _const_index(j, 0, 0)),
            pl.BlockSpec((None, 1, d), _const_index(3 * layer + 2, 0, 0)),
            pl.BlockSpec((d, 2 * d_ff), _const_index(0, 0), pipeline_mode=pl.Buffered(1)),
            pl.BlockSpec((d_ff, d), _const_index(0, 0), pipeline_mode=pl.Buffered(1)),
            pl.BlockSpec((1, d), _const_index(0, 0)),
            *cast_in_specs,
        ],
        out_specs=[pl.BlockSpec((None, tile, d), ffn_tile), *cast_out_specs],
        out_shape=[jax.ShapeDtypeStruct(x.shape, x.dtype), *cast_out_shapes],
        scratch_shapes=[
            pltpu.VMEM((tile, d), F32),
            pltpu.VMEM((tile, d), BF16),
            pltpu.VMEM((tile, d), F32),
            pltpu.VMEM((tile, d_ff), BF16),
            pltpu.VMEM(_slab_plane_shape(tile + 2 * halo, d), F32),
            pltpu.VMEM((tile, d), BF16),
        ],
        compiler_params=pltpu.CompilerParams(
            dimension_semantics=("arbitrary",), vmem_limit_bytes=_vmem_limit(resident)),
        name=f"pool_ffn_l{layer}",
    )(x, x, x, norm_g, pool_w, pool_b, pool_scale, norm_g, w_in, w_out, out_g,
      *(next_ffn[:2] if next_ffn else ()))
    return tuple(outs) if next_ffn else (outs[0], None, None)


def _conv_kernel(xp_ref, x_ref, xn_ref, g_ref, w1_ref, b1_ref, wdw_ref, bdw_ref, lng_ref, lnb_ref,
                 w2_ref, b2_ref, o_ref, h_ref, glu_ref, conv_ref, act_ref):
    tile, d = x_ref.shape
    n_taps = wdw_ref.shape[0]
    i = pl.program_id(1)
    last = pl.num_programs(1) - 1
    g = g_ref[...]
    halo = CONV_HALO
    lw = V7X_LANES
    spp = SLABS_PER_PLANE
    h_ref[0:halo, :] = _rmsnorm(xp_ref[...], g).astype(BF16)
    h_ref[halo:halo + tile, :] = _rmsnorm(x_ref[...], g).astype(BF16)
    h_ref[halo + tile:, :] = _rmsnorm(xn_ref[...], g).astype(BF16)

    h = h_ref[...]
    for c in range(d // (2 * lw)):
        lo = 2 * lw * c
        a = jnp.dot(h, w1_ref[:, lo:lo + 2 * lw], preferred_element_type=F32) + b1_ref[:, lo:lo + 2 * lw]
        gate = (jnp.dot(h, w1_ref[:, d + lo:d + lo + 2 * lw], preferred_element_type=F32)
                + b1_ref[:, d + lo:d + lo + 2 * lw])
        glu = a * jax.nn.sigmoid(gate)
        _store_slabs(glu_ref, 2 * c, ((0, jnp.where(i == 0, 0.0, glu[:halo])),
                                      (halo, glu[halo:halo + tile]),
                                      (halo + tile, jnp.where(i == last, 0.0, glu[halo + tile:]))))

    first = halo - n_taps // 2
    rc = CONV_ROW_CHUNK
    sub = V7X_SUBLANES
    for plane in range(d // (spp * lw)):
        slab_lanes = [slice((plane * spp + p) * lw, (plane * spp + p + 1) * lw) for p in range(spp)]
        for r0 in range(0, tile, rc):
            acc = [jnp.broadcast_to(bdw_ref[:, slab_lanes[p]], (rc // sub, sub, lw)) for p in range(spp)]
            for k in range(n_taps):
                for p in range(spp):
                    window = _load_slab_rows(glu_ref, plane * spp + p, r0 + first + k, rc)
                    tap = jnp.broadcast_to(wdw_ref[k:k + 1, slab_lanes[p]], (sub, lw))
                    acc[p] = acc[p] + window.reshape(rc // sub, sub, lw) * tap
            for p in range(spp):
                conv_ref[r0:r0 + rc, slab_lanes[p]] = acc[p].reshape(rc, lw)

    lc = LN_ROW_CHUNK
    for r0 in range(0, tile, lc):
        v = conv_ref[r0:r0 + lc, :]
        mu = jnp.mean(v, axis=-1, keepdims=True)
        xc = v - mu
        var = jnp.mean(xc * xc, axis=-1, keepdims=True)
        v = xc * lax.rsqrt(var + LN_EPS) * lng_ref[...] + lnb_ref[...]
        act_ref[r0:r0 + lc, :] = (v * jax.nn.sigmoid(v)).astype(BF16)
    y = jnp.dot(act_ref[...], w2_ref[...], preferred_element_type=F32) + b2_ref[...]
    o_ref[...] = x_ref[...] + y


def _conv_mixer(x, norm_g, w_pw1, b_pw1, w_dw, b_dw, ln_g, ln_b, w_pw2, b_pw2, layer, j):
    b, s, d = x.shape
    tile = CONV_TOKEN_TILE
    halo = CONV_HALO
    n_taps = w_dw.shape[1]
    assert s % tile == 0 and tile % halo == 0 and n_taps // 2 <= halo
    assert tile % CONV_ROW_CHUNK == 0 and tile % LN_ROW_CHUNK == 0
    assert d % (SLABS_PER_PLANE * V7X_LANES) == 0 and SLABS_PER_PLANE % 2 == 0
    prev_halo, cur, next_halo = _halo_index_maps(tile, halo, s, lambda bi, i: (bi, i))
    row_spec = lambda width: pl.BlockSpec((None, 1, width), _const_index(j, 0, 0))
    resident = ((w_pw1[0].size + w_pw2[0].size) * 2 + 2 * 2 * tile * d * 4
                + (tile + 2 * halo) * d * (2 + 4) + tile * d * (4 + 2))
    return pl.pallas_call(
        _conv_kernel,
        grid=(b, s // tile),
        in_specs=[
            pl.BlockSpec((None, halo, d), prev_halo),
            pl.BlockSpec((None, tile, d), cur),
            pl.BlockSpec((None, halo, d), next_halo),
            pl.BlockSpec((None, 1, d), _const_index(3 * layer + 1, 0, 0)),
            pl.BlockSpec((None, d, 2 * d), _const_index(j, 0, 0), pipeline_mode=pl.Buffered(1)),
            row_spec(2 * d),
            pl.BlockSpec((None, n_taps, d), _const_index(j, 0, 0)),
            row_spec(d),
            row_spec(d),
            row_spec(d),
            pl.BlockSpec((None, d, d), _const_index(j, 0, 0), pipeline_mode=pl.Buffered(1)),
            row_spec(d),
        ],
        out_specs=pl.BlockSpec((None, tile, d), cur),
        out_shape=jax.ShapeDtypeStruct(x.shape, x.dtype),
        scratch_shapes=[
            pltpu.VMEM((tile + 2 * halo, d), BF16),
            pltpu.VMEM(_slab_plane_shape(tile + 2 * halo, d), F32),
            pltpu.VMEM((tile, d), F32),
            pltpu.VMEM((tile, d), BF16),
        ],
        compiler_params=pltpu.CompilerParams(
            dimension_semantics=("arbitrary", "arbitrary"), vmem_limit_bytes=_vmem_limit(resident)),
        name=f"conv_mixer_l{layer}",
    )(x, x, x, norm_g, w_pw1, b_pw1, w_dw, b_dw, ln_g, ln_b, w_pw2, b_pw2)


def kernel(x, norm_g, ffn_w_in, ffn_w_out, pool_w, pool_b, pool_scale, conv_w_pw1, conv_b_pw1, conv_w_dw,
           conv_b_dw, conv_ln_g, conv_ln_b, conv_w_pw2, conv_b_pw2, final_g):
    b, s, d = x.shape
    depth = norm_g.shape[0]
    row3 = lambda a: a.reshape(a.shape[0], 1, a.shape[1])
    norm_rows = norm_g.reshape(depth * 3, 1, d)
    pool_w16 = pool_w.astype(BF16)
    w_pw1 = conv_w_pw1.astype(BF16)
    w_pw2 = conv_w_pw2.astype(BF16)
    out_g = final_g.reshape(1, d)
    assert depth >= 1

    w_in, w_out = ffn_w_in[0, 0].astype(BF16), ffn_w_out[0, 0].astype(BF16)

    def next_ffn(layer, which):
        layer, which = (layer, 1) if which == 0 else (layer + 1, 0)
        return (ffn_w_in, ffn_w_out, layer, which) if layer < depth else None

    for i in range(depth):
        x, w_in, w_out = _ffn(x.reshape(b * s, d), norm_rows, w_in, w_out, out_g, i, 0, False, next_ffn(i, 0))
        x = x.reshape(b, s, d)
        j = i // N_MIXERS
        last_layer = i == depth - 1
        if i % N_MIXERS == 0:
            x, w_in, w_out = _pool_ffn(x, norm_rows, pool_w16, row3(pool_b), row3(pool_scale), w_in, w_out,
                                       out_g, i, j, last_layer, next_ffn(i, 1))
        else:
            x = _conv_mixer(x, norm_rows, w_pw1, row3(conv_b_pw1), conv_w_dw, row3(conv_b_dw),
                            row3(conv_ln_g), row3(conv_ln_b), w_pw2, row3(conv_b_pw2), i, j)
            x, w_in, w_out = _ffn(x.reshape(b * s, d), norm_rows, w_in, w_out, out_g, i, 1, last_layer,
                                  next_ffn(i, 1))
            x = x.reshape(b, s, d)
    return x
```

```python
import functools

import jax
import jax.numpy as jnp
from jax import lax
from jax.experimental import pallas as pl
from jax.experimental.pallas import tpu as pltpu

POOL_WINDOWS = (2, 4, 8, 16)
N_MIXERS = 2
RMS_EPS = 1e-6
LN_EPS = 1e-5
FFN_RESIDUAL_WEIGHT = 0.5

V7X_MXU_DIM = 256
V7X_BF16_SUBLANES = 16
V7X_LANES = 128
V7X_SUBLANES = 8
V7X_VMEM_BYTES = 64 * 2**20

FFN_TOKEN_TILE = 1024
CONV_TOKEN_TILE = 1024
POOL_FFN_TOKEN_TILE = 512
FFN_HIDDEN_CHUNK = V7X_MXU_DIM
CAST_AHEAD_MAX_CHUNKS = 16
NORM_NEXT_OUT_CHUNKS = 2
SIDE_WORK_LAG = 4
POOL_HALO = 8
CONV_HALO = V7X_BF16_SUBLANES
SLABS_PER_PLANE = 4
POOL_ROW_CHUNK = 128
CONV_ROW_CHUNK = 64
LN_ROW_CHUNK = 64

BF16 = jnp.bfloat16
F32 = jnp.float32


def _rmsnorm(x, g):
    return x * lax.rsqrt(jnp.mean(x * x, axis=-1, keepdims=True) + RMS_EPS) * g


def _vmem_limit(resident_bytes):
    return int(min(resident_bytes + 24 * 2**20, V7X_VMEM_BYTES - 4 * 2**20))


def _const_index(*idx):
    return lambda *grid_indices: idx


def _slab_plane_shape(rows, d):
    return (d // (SLABS_PER_PLANE * V7X_LANES), SLABS_PER_PLANE * rows, V7X_LANES)


def _store_slabs(dst_ref, first_slab, pieces):
    lw = V7X_LANES
    for row0, piece in pieces:
        for s in range(piece.shape[1] // lw):
            plane, phase = divmod(first_slab + s, SLABS_PER_PLANE)
            dst_ref[plane, pl.ds(SLABS_PER_PLANE * row0 + phase, piece.shape[0], stride=SLABS_PER_PLANE), :] = (
                piece[:, s * lw:(s + 1) * lw])


def _load_slab_rows(src_ref, slab, row0, n_rows):
    plane, phase = divmod(slab, SLABS_PER_PLANE)
    return src_ref[plane, pl.ds(SLABS_PER_PLANE * row0 + phase, n_rows, stride=SLABS_PER_PLANE), :]


def _halo_index_maps(tile, halo, seq_len, tile_of_step):
    blocks_per_tile = tile // halo
    last_block = seq_len // halo - 1

    def prev_halo(*g):
        bi, i = tile_of_step(*g)
        return bi, jnp.maximum(i * blocks_per_tile - 1, 0), 0

    def cur(*g):
        bi, i = tile_of_step(*g)
        return bi, i, 0

    def next_halo(*g):
        bi, i = tile_of_step(*g)
        return bi, jnp.minimum((i + 1) * blocks_per_tile, last_block), 0

    return prev_halo, cur, next_halo


def _zero_after(v):
    bits = lax.bitcast_convert_type(v, jnp.uint32)
    zero = lax.shift_right_logical(lax.shift_right_logical(bits, jnp.uint32(16)), jnp.uint32(16))
    return lax.bitcast_convert_type(zero, F32)


def _ffn_slots(d, d_ff):
    n_in = 2 * (d_ff // FFN_HIDDEN_CHUNK)
    return n_in, n_in + d // FFN_HIDDEN_CHUNK


def _interleave_gate_up(w_in):
    d, two_ff = w_in.shape
    n = two_ff // (2 * FFN_HIDDEN_CHUNK)
    return w_in.reshape(d, 2, n, FFN_HIDDEN_CHUNK).transpose(0, 2, 1, 3).reshape(d, two_ff)


def _ffn_body(h, x_res, win_ref, wout_ref, act_ref, side_work=None, merged_gate_up=False):
    d_ff, d = wout_ref.shape
    chunk = FFN_HIDDEN_CHUNK
    n_in, _ = _ffn_slots(d, d_ff)
    side_work = side_work or {}

    def run_side_work(slot, result):
        for work in side_work.get(slot, ()):
            work(_zero_after(result[:V7X_SUBLANES, :V7X_LANES]))

    for c in range(d_ff // chunk):
        lo = c * chunk
        if merged_gate_up:
            both = jnp.dot(h[...], win_ref[:, 2 * lo:2 * lo + 2 * chunk], preferred_element_type=F32)
            gate, up = both[:, :chunk], both[:, chunk:]
            run_side_work(2 * c, gate)
        else:
            gate = jnp.dot(h[...], win_ref[:, 2 * lo:2 * lo + chunk], preferred_element_type=F32)
            run_side_work(2 * c, gate)
            up = jnp.dot(h[...], win_ref[:, 2 * lo + chunk:2 * lo + 2 * chunk], preferred_element_type=F32)
        act_ref[:, lo:lo + chunk] = (gate * jax.nn.sigmoid(gate) * up).astype(BF16)
        run_side_work(2 * c + 1, up)
    out = []
    for c in range(d // chunk):
        lo = c * chunk
        y = jnp.dot(act_ref[...], wout_ref[:, lo:lo + chunk], preferred_element_type=F32)
        out.append(x_res[:, lo:lo + chunk] + FFN_RESIDUAL_WEIGHT * y)
        run_side_work(n_in + c, y)
    return jnp.concatenate(out, axis=-1)


def _cast_ahead_plumbing(next_ffn, n_steps, chunk_step):
    w_in_all, w_out_all, layer, which = next_ffn
    in_specs, out_specs, out_shapes = [], [], []
    for w_all in (w_in_all, w_out_all):
        rows, cols = w_all.shape[2:]
        n_chunks = max(n for n in range(1, CAST_AHEAD_MAX_CHUNKS + 1)
                       if n_steps % n == 0 and rows % (n * V7X_BF16_SUBLANES) == 0)
        chunk_rows, steps_per_chunk = rows // n_chunks, n_steps // n_chunks
        in_specs.append(pl.BlockSpec(
            (None, None, chunk_rows, cols),
            lambda t, k=steps_per_chunk: (layer, which, chunk_step(t) // k, 0)))
        out_specs.append(pl.BlockSpec((chunk_rows, cols), lambda t, k=steps_per_chunk: (chunk_step(t) // k, 0)))
        out_shapes.append(jax.ShapeDtypeStruct((rows, cols), BF16))
    return in_specs, out_specs, out_shapes


def _cast_ahead(src_refs, dst_refs, sync):
    (in_src, out_src), (in_dst, out_dst) = src_refs, dst_refs
    cast = lambda v: (v + jnp.tile(sync, (v.shape[0] // V7X_SUBLANES, v.shape[1] // V7X_LANES))).astype(BF16)
    out_dst[...] = cast(out_src[...])
    chunk = FFN_HIDDEN_CHUNK
    d_ff = in_src.shape[1] // 2
    for c in range(d_ff // chunk):
        in_dst[:, 2 * c * chunk:(2 * c + 1) * chunk] = cast(in_src[:, c * chunk:(c + 1) * chunk])
        in_dst[:, (2 * c + 1) * chunk:(2 * c + 2) * chunk] = cast(in_src[:, d_ff + c * chunk:d_ff + (c + 1) * chunk])


def _ffn_kernel(x_ref, xnext_ref, g_ref, win_ref, wout_ref, og_ref, *rest, out_norm, cast_next):
    n_in, _ = _ffn_slots(*wout_ref.shape[::-1])
    side_work = {slot: [] for slot in range(n_in + NORM_NEXT_OUT_CHUNKS)}
    if cast_next:
        *next_f32_refs, o_ref, next_in_ref, next_out_ref, h_ref, act_ref = rest
        side_work[n_in // 2].append(functools.partial(_cast_ahead, next_f32_refs, (next_in_ref, next_out_ref)))
    else:
        o_ref, h_ref, act_ref = rest

    @pl.when(pl.program_id(0) == 0)
    def _():
        h_ref[...] = _rmsnorm(x_ref[...], g_ref[...]).astype(BF16)

    h_next = []

    def norm_next(sync):
        x = xnext_ref[...]
        mean_sq = jnp.mean(x * x, axis=-1, keepdims=True) + jnp.tile(sync[:, :1], (x.shape[0] // V7X_SUBLANES, 1))
        h_next.append((x * lax.rsqrt(mean_sq + RMS_EPS) * g_ref[...]).astype(BF16))
        h_ref[...] = h_next[0]

    def norm_next_deadline(sync):
        del sync
        rows, lanes = V7X_BF16_SUBLANES, V7X_LANES
        h = h_next[0]
        every = functools.reduce(lambda a, b: a + b, [h[r:r + rows] for r in range(0, h.shape[0], rows)])
        every = functools.reduce(lambda a, b: a + b, [every[:, c:c + lanes] for c in range(0, h.shape[1], lanes)])
        done = _zero_after(every.astype(F32)).astype(BF16)
        act_ref[0:rows, 0:lanes] = act_ref[0:rows, 0:lanes] + done

    side_work[n_in - 1].append(norm_next)
    side_work[n_in + NORM_NEXT_OUT_CHUNKS - 1].append(norm_next_deadline)
    out = _ffn_body(h_ref, x_ref[...], win_ref, wout_ref, act_ref, side_work, merged_gate_up=True)
    o_ref[...] = _rmsnorm(out, og_ref[...]) if out_norm else out


def _ffn(x2d, norm_g, w_in, w_out, out_g, layer, which, out_norm, next_ffn):
    n_tok, d = x2d.shape
    d_ff = w_out.shape[0]
    tm = FFN_TOKEN_TILE
    assert n_tok % tm == 0 and d_ff % FFN_HIDDEN_CHUNK == 0
    n_steps = n_tok // tm
    cast_in_specs, cast_out_specs, cast_out_shapes = (
        _cast_ahead_plumbing(next_ffn, n_steps, lambda t: t) if next_ffn else ([], [], []))
    cast_bytes = 3 * (w_in.size + w_out.size) * 2 * 2 // n_steps
    resident = ((3 * 2 * tm * d * 4) + (w_in.size + w_out.size) * 2 + tm * d_ff * 2 + tm * d * 2 + cast_bytes)
    outs = pl.pallas_call(
        functools.partial(_ffn_kernel, out_norm=out_norm, cast_next=bool(next_ffn)),
        grid=(n_steps,),
        in_specs=[
            pl.BlockSpec((tm, d), lambda t: (t, 0)),
            pl.BlockSpec((tm, d), lambda t: (jnp.minimum(t + 1, n_steps - 1), 0)),
            pl.BlockSpec((None, 1, d), _const_index(3 * layer + 2 * which, 0, 0)),
            pl.BlockSpec((d, 2 * d_ff), _const_index(0, 0), pipeline_mode=pl.Buffered(1)),
            pl.BlockSpec((d_ff, d), _const_index(0, 0), pipeline_mode=pl.Buffered(1)),
            pl.BlockSpec((1, d), _const_index(0, 0)),
            *cast_in_specs,
        ],
        out_specs=[pl.BlockSpec((tm, d), lambda t: (t, 0)), *cast_out_specs],
        out_shape=[jax.ShapeDtypeStruct((n_tok, d), x2d.dtype), *cast_out_shapes],
        scratch_shapes=[pltpu.VMEM((tm, d), BF16), pltpu.VMEM((tm, d_ff), BF16)],
        compiler_params=pltpu.CompilerParams(
            dimension_semantics=("arbitrary",), vmem_limit_bytes=_vmem_limit(resident)),
        name=f"ffn_l{layer}_{which}",
    )(x2d, x2d, norm_g, w_in, w_out, out_g, *(next_ffn[:2] if next_ffn else ()))
    return tuple(outs) if next_ffn else (outs[0], None, None)


def _pool_ffn_kernel(xp_ref, x_ref, xn_ref, gm_ref, w_ref, b_ref, s_ref, gf_ref, win_ref, wout_ref, og_ref,
                     *rest, seq_len, tiles_per_seq, out_norm, cast_next):
    cast_work = []
    if cast_next:
        *next_f32_refs, o_ref, next_in_ref, next_out_ref = rest[:5]
        cast_work.append(functools.partial(_cast_ahead, next_f32_refs, (next_in_ref, next_out_ref)))
        rest = rest[5:]
    else:
        o_ref, rest = rest[0], rest[1:]
    xmid_ref, hmid_ref, xres_ref, act_ref, h_ref, pooled_ref = rest
    tile, d = x_ref.shape
    halo = POOL_HALO
    lw = V7X_LANES
    gd = w_ref.shape[-1]
    rc = POOL_ROW_CHUNK
    t = pl.program_id(0)
    n_tiles = pl.num_programs(0) - 1
    i = lax.rem(jnp.minimum(t, n_tiles - 1), tiles_per_seq)

    @pl.when(t == 0)
    def _():
        xmid_ref[...] = jnp.zeros(xmid_ref.shape, F32)
        hmid_ref[...] = jnp.zeros(hmid_ref.shape, BF16)

    def norm_to_slabs(sync):
        del sync
        g = gm_ref[...]
        _store_slabs(h_ref, 0, ((0, jnp.where(i == 0, 0.0, _rmsnorm(xp_ref[...], g))),
                                (halo, _rmsnorm(x_ref[...], g)),
                                (halo + tile, jnp.where(i == tiles_per_seq - 1, 0.0, _rmsnorm(xn_ref[...], g)))))

    def pool_rows(r0, sync):
        sync_rows = jnp.tile(sync, (rc // V7X_SUBLANES, 1))
        pos = i * tile + r0 + lax.broadcasted_iota(jnp.int32, (rc, 1), 0)
        for gi, w in enumerate(POOL_WINDOWS):
            lo = pos - w // 2
            cnt = (jnp.minimum(lo + w, seq_len) - jnp.maximum(lo, 0)).astype(F32)
            inv_cnt = 1.0 / cnt
            for slab in range(gi * gd // lw, (gi + 1) * gd // lw):
                win_sum = _load_slab_rows(h_ref, slab, halo + r0 - w // 2, rc) + sync_rows
                for k in range(1, w):
                    win_sum = win_sum + _load_slab_rows(h_ref, slab, halo + r0 - w // 2 + k, rc)
                pooled = win_sum * inv_cnt - _load_slab_rows(h_ref, slab, halo + r0, rc)
                pooled_ref[r0:r0 + rc, slab * lw:(slab + 1) * lw] = pooled.astype(BF16)

    def mix_rows(r0, sync):
        del sync
        for gi in range(len(POOL_WINDOWS)):
            lanes = slice(gi * gd, (gi + 1) * gd)
            y = (jnp.dot(pooled_ref[r0:r0 + rc, lanes], w_ref[gi], preferred_element_type=F32)
                 + b_ref[:, lanes])
            xmid_ref[r0:r0 + rc, lanes] = x_ref[r0:r0 + rc, lanes] + y * s_ref[:, lanes]

    def norm_mixed(sync):
        del sync
        hmid_ref[...] = _rmsnorm(xmid_ref[...], gf_ref[...]).astype(BF16)

    n_in, n_slots = _ffn_slots(d, wout_ref.shape[0])
    row_chunks = list(range(0, tile, rc))
    spacing = (n_in - 2 - SIDE_WORK_LAG) // len(row_chunks)
    assert spacing >= 1
    side_work = {slot: [] for slot in range(n_slots)}
    side_work[0].append(norm_to_slabs)
    for k, r0 in enumerate(row_chunks):
        side_work[1 + k * spacing].append(functools.partial(pool_rows, r0))
        side_work[1 + k * spacing + SIDE_WORK_LAG].append(functools.partial(mix_rows, r0))
    side_work[n_in - 1].append(norm_mixed)
    side_work[n_in].extend(cast_work)

    xres_ref[...] = xmid_ref[...]
    out = _ffn_body(hmid_ref[...], xres_ref[...], win_ref, wout_ref, act_ref, side_work)
    o_ref[...] = _rmsnorm(out, og_ref[...]) if out_norm else out


def _pool_ffn(x, norm_g, pool_w, pool_b, pool_scale, w_in, w_out, out_g, layer, j, out_norm, next_ffn):
    b, s, d = x.shape
    d_ff = w_out.shape[0]
    tile = POOL_FFN_TOKEN_TILE
    halo = POOL_HALO
    assert s % tile == 0 and tile % halo == 0 and tile % POOL_ROW_CHUNK == 0 and d_ff % FFN_HIDDEN_CHUNK == 0
    assert d % (SLABS_PER_PLANE * V7X_LANES) == 0 and pool_w.shape[-1] % V7X_LANES == 0
    assert max(POOL_WINDOWS) // 2 <= halo
    tiles_per_seq = s // tile
    n_tiles = b * tiles_per_seq

    def mixer_tile(t):
        tb = jnp.minimum(t, n_tiles - 1)
        return tb // tiles_per_seq, tb % tiles_per_seq

    def ffn_tile(t):
        ta = jnp.maximum(t - 1, 0)
        return ta // tiles_per_seq, ta % tiles_per_seq, 0

    prev_halo, cur, next_halo = _halo_index_maps(tile, halo, s, mixer_tile)
    cast_in_specs, cast_out_specs, cast_out_shapes = (
        _cast_ahead_plumbing(next_ffn, n_tiles, lambda t: jnp.minimum(t, n_tiles - 1)) if next_ffn
        else ([], [], []))
    cast_bytes = 3 * (w_in.size + w_out.size) * 2 * 2 // n_tiles
    resident = ((w_in.size + w_out.size) * 2 + 2 * pool_w[0].size * 2 + 2 * 2 * tile * d * 4
                + 2 * tile * d * 4 + 2 * tile * d * 2 + tile * d_ff * 2 + (tile + 2 * halo) * d * 4 + cast_bytes)
    outs = pl.pallas_call(
        functools.partial(_pool_ffn_kernel, seq_len=s, tiles_per_seq=tiles_per_seq, out_norm=out_norm,
                          cast_next=bool(next_ffn)),
        grid=(n_tiles + 1,),
        in_specs=[
            pl.BlockSpec((None, halo, d), prev_halo),
            pl.BlockSpec((None, tile, d), cur),
            pl.BlockSpec((None, halo, d), next_halo),
            pl.BlockSpec((None, 1, d), _const_index(3 * layer + 1, 0, 0)),
            pl.BlockSpec((None,) + pool_w.shape[1:], _const_index(j, 0, 0, 0)),
            pl.BlockSpec((None, 1, d), _const_index(j, 0, 0)),
            pl.BlockSpec((None, 1, d), _const_index(j, 0, 0)),
            pl.BlockSpec((None, 1, d), _const_index(3 * layer + 2, 0, 0)),
            pl.BlockSpec((d, 2 * d_ff), _const_index(0, 0), pipeline_mode=pl.Buffered(1)),
            pl.BlockSpec((d_ff, d), _const_index(0, 0), pipeline_mode=pl.Buffered(1)),
            pl.BlockSpec((1, d), _const_index(0, 0)),
            *cast_in_specs,
        ],
        out_specs=[pl.BlockSpec((None, tile, d), ffn_tile), *cast_out_specs],
        out_shape=[jax.ShapeDtypeStruct(x.shape, x.dtype), *cast_out_shapes],
        scratch_shapes=[
            pltpu.VMEM((tile, d), F32),
            pltpu.VMEM((tile, d), BF16),
            pltpu.VMEM((tile, d), F32),
            pltpu.VMEM((tile, d_ff), BF16),
            pltpu.VMEM(_slab_plane_shape(tile + 2 * halo, d), F32),
            pltpu.VMEM((tile, d), BF16),
        ],
        compiler_params=pltpu.CompilerParams(
            dimension_semantics=("arbitrary",), vmem_limit_bytes=_vmem_limit(resident)),
        name=f"pool_ffn_l{layer}",
    )(x, x, x, norm_g, pool_w, pool_b, pool_scale, norm_g, w_in, w_out, out_g,
      *(next_ffn[:2] if next_ffn else ()))
    return tuple(outs) if next_ffn else (outs[0], None, None)


def _conv_kernel(xp_ref, x_ref, xn_ref, g_ref, w1_ref, b1_ref, wdw_ref, bdw_ref, lng_ref, lnb_ref,
                 w2_ref, b2_ref, o_ref, h_ref, glu_ref, conv_ref, act_ref):
    tile, d = x_ref.shape
    n_taps = wdw_ref.shape[0]
    i = pl.program_id(1)
    last = pl.num_programs(1) - 1
    g = g_ref[...]
    halo = CONV_HALO
    lw = V7X_LANES
    spp = SLABS_PER_PLANE
    h_ref[0:halo, :] = _rmsnorm(xp_ref[...], g).astype(BF16)
    h_ref[halo:halo + tile, :] = _rmsnorm(x_ref[...], g).astype(BF16)
    h_ref[halo + tile:, :] = _rmsnorm(xn_ref[...], g).astype(BF16)

    h = h_ref[...]
    for c in range(d // (2 * lw)):
        lo = 2 * lw * c
        a = jnp.dot(h, w1_ref[:, lo:lo + 2 * lw], preferred_element_type=F32) + b1_ref[:, lo:lo + 2 * lw]
        gate = (jnp.dot(h, w1_ref[:, d + lo:d + lo + 2 * lw], preferred_element_type=F32)
                + b1_ref[:, d + lo:d + lo + 2 * lw])
        glu = a * jax.nn.sigmoid(gate)
        _store_slabs(glu_ref, 2 * c, ((0, jnp.where(i == 0, 0.0, glu[:halo])),
                                      (halo, glu[halo:halo + tile]),
                                      (halo + tile, jnp.where(i == last, 0.0, glu[halo + tile:]))))

    first = halo - n_taps // 2
    rc = CONV_ROW_CHUNK
    sub = V7X_SUBLANES
    for plane in range(d // (spp * lw)):
        slab_lanes = [slice((plane * spp + p) * lw, (plane * spp + p + 1) * lw) for p in range(spp)]
        for r0 in range(0, tile, rc):
            acc = [jnp.broadcast_to(bdw_ref[:, slab_lanes[p]], (rc // sub, sub, lw)) for p in range(spp)]
            for k in range(n_taps):
                for p in range(spp):
                    window = _load_slab_rows(glu_ref, plane * spp + p, r0 + first + k, rc)
                    tap = jnp.broadcast_to(wdw_ref[k:k + 1, slab_lanes[p]], (sub, lw))
                    acc[p] = acc[p] + window.reshape(rc // sub, sub, lw) * tap
            for p in range(spp):
                conv_ref[r0:r0 + rc, slab_lanes[p]] = acc[p].reshape(rc, lw)

    lc = LN_ROW_CHUNK
    for r0 in range(0, tile, lc):
        v = conv_ref[r0:r0 + lc, :]
        mu = jnp.mean(v, axis=-1, keepdims=True)
        xc = v - mu
        var = jnp.mean(xc * xc, axis=-1, keepdims=True)
        v = xc * lax.rsqrt(var + LN_EPS) * lng_ref[...] + lnb_ref[...]
        act_ref[r0:r0 + lc, :] = (v * jax.nn.sigmoid(v)).astype(BF16)
    y = jnp.dot(act_ref[...], w2_ref[...], preferred_element_type=F32) + b2_ref[...]
    o_ref[...] = x_ref[...] + y


def _conv_mixer(x, norm_g, w_pw1, b_pw1, w_dw, b_dw, ln_g, ln_b, w_pw2, b_pw2, layer, j):
    b, s, d = x.shape
    tile = CONV_TOKEN_TILE
    halo = CONV_HALO
    n_taps = w_dw.shape[1]
    assert s % tile == 0 and tile % halo == 0 and n_taps // 2 <= halo
    assert tile % CONV_ROW_CHUNK == 0 and tile % LN_ROW_CHUNK == 0
    assert d % (SLABS_PER_PLANE * V7X_LANES) == 0 and SLABS_PER_PLANE % 2 == 0
    prev_halo, cur, next_halo = _halo_index_maps(tile, halo, s, lambda bi, i: (bi, i))
    row_spec = lambda width: pl.BlockSpec((None, 1, width), _const_index(j, 0, 0))
    resident = ((w_pw1[0].size + w_pw2[0].size) * 2 + 2 * 2 * tile * d * 4
                + (tile + 2 * halo) * d * (2 + 4) + tile * d * (4 + 2))
    return pl.pallas_call(
        _conv_kernel,
        grid=(b, s // tile),
        in_specs=[
            pl.BlockSpec((None, halo, d), prev_halo),
            pl.BlockSpec((None, tile, d), cur),
            pl.BlockSpec((None, halo, d), next_halo),
            pl.BlockSpec((None, 1, d), _const_index(3 * layer + 1, 0, 0)),
            pl.BlockSpec((None, d, 2 * d), _const_index(j, 0, 0), pipeline_mode=pl.Buffered(1)),
            row_spec(2 * d),
            pl.BlockSpec((None, n_taps, d), _const_index(j, 0, 0)),
            row_spec(d),
            row_spec(d),
            row_spec(d),
            pl.BlockSpec((None, d, d), _const_index(j, 0, 0), pipeline_mode=pl.Buffered(1)),
            row_spec(d),
        ],
        out_specs=pl.BlockSpec((None, tile, d), cur),
        out_shape=jax.ShapeDtypeStruct(x.shape, x.dtype),
        scratch_shapes=[
            pltpu.VMEM((tile + 2 * halo, d), BF16),
            pltpu.VMEM(_slab_plane_shape(tile + 2 * halo, d), F32),
            pltpu.VMEM((tile, d), F32),
            pltpu.VMEM((tile, d), BF16),
        ],
        compiler_params=pltpu.CompilerParams(
            dimension_semantics=("arbitrary", "arbitrary"), vmem_limit_bytes=_vmem_limit(resident)),
        name=f"conv_mixer_l{layer}",
    )(x, x, x, norm_g, w_pw1, b_pw1, w_dw, b_dw, ln_g, ln_b, w_pw2, b_pw2)


def kernel(x, norm_g, ffn_w_in, ffn_w_out, pool_w, pool_b, pool_scale, conv_w_pw1, conv_b_pw1, conv_w_dw,
           conv_b_dw, conv_ln_g, conv_ln_b, conv_w_pw2, conv_b_pw2, final_g):
    b, s, d = x.shape
    depth = norm_g.shape[0]
    row3 = lambda a: a.reshape(a.shape[0], 1, a.shape[1])
    norm_rows = norm_g.reshape(depth * 3, 1, d)
    pool_w16 = pool_w.astype(BF16)
    w_pw1 = conv_w_pw1.astype(BF16)
    w_pw2 = conv_w_pw2.astype(BF16)
    out_g = final_g.reshape(1, d)
    assert depth >= 1

    w_in, w_out = _interleave_gate_up(ffn_w_in[0, 0]).astype(BF16), ffn_w_out[0, 0].astype(BF16)

    def next_ffn(layer, which):
        layer, which = (layer, 1) if which == 0 else (layer + 1, 0)
        return (ffn_w_in, ffn_w_out, layer, which) if layer < depth else None

    for i in range(depth):
        x, w_in, w_out = _ffn(x.reshape(b * s, d), norm_rows, w_in, w_out, out_g, i, 0, False, next_ffn(i, 0))
        x = x.reshape(b, s, d)
        j = i // N_MIXERS
        last_layer = i == depth - 1
        if i % N_MIXERS == 0:
            x, w_in, w_out = _pool_ffn(x, norm_rows, pool_w16, row3(pool_b), row3(pool_scale), w_in, w_out,
                                       out_g, i, j, last_layer, next_ffn(i, 1))
        else:
            x = _conv_mixer(x, norm_rows, w_pw1, row3(conv_b_pw1), conv_w_dw, row3(conv_b_dw),
                            row3(conv_ln_g), row3(conv_ln_b), w_pw2, row3(conv_b_pw2), i, j)
            x, w_in, w_out = _ffn(x.reshape(b * s, d), norm_rows, w_in, w_out, out_g, i, 1, last_layer,
                                  next_ffn(i, 1))
            x = x.reshape(b, s, d)
    return x
```

```python
import functools

import jax
import jax.numpy as jnp
from jax import lax
from jax.experimental import pallas as pl
from jax.experimental.pallas import tpu as pltpu

POOL_WINDOWS = (2, 4, 8, 16)
N_MIXERS = 2
RMS_EPS = 1e-6
LN_EPS = 1e-5
FFN_RESIDUAL_WEIGHT = 0.5

V7X_MXU_DIM = 256
V7X_BF16_SUBLANES = 16
V7X_LANES = 128
V7X_SUBLANES = 8
V7X_VMEM_BYTES = 64 * 2**20

FFN_TOKEN_TILE = 1024
CONV_TOKEN_TILE = 1024
POOL_FFN_TOKEN_TILE = 512
FFN_HIDDEN_CHUNK = V7X_MXU_DIM
NORM_NEXT_OUT_CHUNKS = 2
SIDE_WORK_LAG = 4
POOL_HALO = 8
CONV_HALO = V7X_BF16_SUBLANES
SLABS_PER_PLANE = 4
POOL_ROW_CHUNK = 128
CONV_ROW_CHUNK = 64
LN_ROW_CHUNK = 64

BF16 = jnp.bfloat16
F32 = jnp.float32


def _rmsnorm(x, g):
    return x * lax.rsqrt(jnp.mean(x * x, axis=-1, keepdims=True) + RMS_EPS) * g


def _vmem_limit(resident_bytes):
    return int(min(resident_bytes + 24 * 2**20, V7X_VMEM_BYTES - 4 * 2**20))


def _const_index(*idx):
    return lambda *grid_indices: idx


def _slab_plane_shape(rows, d):
    return (d // (SLABS_PER_PLANE * V7X_LANES), SLABS_PER_PLANE * rows, V7X_LANES)


def _store_slabs(dst_ref, first_slab, pieces):
    lw = V7X_LANES
    for row0, piece in pieces:
        for s in range(piece.shape[1] // lw):
            plane, phase = divmod(first_slab + s, SLABS_PER_PLANE)
            dst_ref[plane, pl.ds(SLABS_PER_PLANE * row0 + phase, piece.shape[0], stride=SLABS_PER_PLANE), :] = (
                piece[:, s * lw:(s + 1) * lw])


def _load_slab_rows(src_ref, slab, row0, n_rows):
    plane, phase = divmod(slab, SLABS_PER_PLANE)
    return src_ref[plane, pl.ds(SLABS_PER_PLANE * row0 + phase, n_rows, stride=SLABS_PER_PLANE), :]


def _halo_index_maps(tile, halo, seq_len, tile_of_step):
    blocks_per_tile = tile // halo
    last_block = seq_len // halo - 1

    def prev_halo(*g):
        bi, i = tile_of_step(*g)
        return bi, jnp.maximum(i * blocks_per_tile - 1, 0), 0

    def cur(*g):
        bi, i = tile_of_step(*g)
        return bi, i, 0

    def next_halo(*g):
        bi, i = tile_of_step(*g)
        return bi, jnp.minimum((i + 1) * blocks_per_tile, last_block), 0

    return prev_halo, cur, next_halo


def _zero_after(v):
    bits = lax.bitcast_convert_type(v, jnp.uint32)
    zero = lax.shift_right_logical(lax.shift_right_logical(bits, jnp.uint32(16)), jnp.uint32(16))
    return lax.bitcast_convert_type(zero, F32)


def _ffn_slots(d, d_ff):
    n_in = 2 * (d_ff // FFN_HIDDEN_CHUNK)
    return n_in, n_in + d // FFN_HIDDEN_CHUNK


def _ffn_body(h, x_res, win_ref, wout_ref, act_ref, side_work=None):
    d_ff, d = wout_ref.shape
    chunk = FFN_HIDDEN_CHUNK
    n_in, _ = _ffn_slots(d, d_ff)
    side_work = side_work or {}

    def run_side_work(slot, result):
        for work in side_work.get(slot, ()):
            work(_zero_after(result[:V7X_SUBLANES, :V7X_LANES]))

    for c in range(d_ff // chunk):
        lo = c * chunk
        gate = jnp.dot(h[...], win_ref[:, lo:lo + chunk], preferred_element_type=F32)
        run_side_work(2 * c, gate)
        up = jnp.dot(h[...], win_ref[:, d_ff + lo:d_ff + lo + chunk], preferred_element_type=F32)
        act_ref[:, lo:lo + chunk] = (gate * jax.nn.sigmoid(gate) * up).astype(BF16)
        run_side_work(2 * c + 1, up)
    out = []
    for c in range(d // chunk):
        lo = c * chunk
        y = jnp.dot(act_ref[...], wout_ref[:, lo:lo + chunk], preferred_element_type=F32)
        out.append(x_res[:, lo:lo + chunk] + FFN_RESIDUAL_WEIGHT * y)
        run_side_work(n_in + c, y)
    return jnp.concatenate(out, axis=-1)


def _cast_ahead_plumbing(next_ffn, n_steps, chunk_step):
    w_in_all, w_out_all, layer, which = next_ffn
    in_specs, out_specs, out_shapes = [], [], []
    for w_all in (w_in_all, w_out_all):
        rows, cols = w_all.shape[2:]
        n_chunks = max(n for n in range(1, n_steps + 1)
                       if n_steps % n == 0 and rows % (n * V7X_BF16_SUBLANES) == 0)
        chunk_rows, steps_per_chunk = rows // n_chunks, n_steps // n_chunks
        in_specs.append(pl.BlockSpec(
            (None, None, chunk_rows, cols),
            lambda t, k=steps_per_chunk: (layer, which, chunk_step(t) // k, 0)))
        out_specs.append(pl.BlockSpec((chunk_rows, cols), lambda t, k=steps_per_chunk: (chunk_step(t) // k, 0)))
        out_shapes.append(jax.ShapeDtypeStruct((rows, cols), BF16))
    return in_specs, out_specs, out_shapes


def _cast_ahead(src_refs, dst_refs, sync):
    for src_ref, dst_ref in zip(src_refs, dst_refs):
        rows, cols = src_ref.shape
        dst_ref[...] = (src_ref[...] + jnp.tile(sync, (rows // V7X_SUBLANES, cols // V7X_LANES))).astype(BF16)


def _ffn_kernel(x_ref, xnext_ref, g_ref, win_ref, wout_ref, og_ref, *rest, out_norm, cast_next):
    n_in, _ = _ffn_slots(*wout_ref.shape[::-1])
    side_work = {slot: [] for slot in range(n_in + NORM_NEXT_OUT_CHUNKS)}
    if cast_next:
        *next_f32_refs, o_ref, next_in_ref, next_out_ref, h_ref, act_ref = rest
        side_work[n_in // 2].append(functools.partial(_cast_ahead, next_f32_refs, (next_in_ref, next_out_ref)))
    else:
        o_ref, h_ref, act_ref = rest

    @pl.when(pl.program_id(0) == 0)
    def _():
        h_ref[...] = _rmsnorm(x_ref[...], g_ref[...]).astype(BF16)

    h_next = []

    def norm_next(sync):
        x = xnext_ref[...]
        mean_sq = jnp.mean(x * x, axis=-1, keepdims=True) + jnp.tile(sync[:, :1], (x.shape[0] // V7X_SUBLANES, 1))
        h_next.append((x * lax.rsqrt(mean_sq + RMS_EPS) * g_ref[...]).astype(BF16))
        h_ref[...] = h_next[0]

    def norm_next_deadline(sync):
        del sync
        rows, lanes = V7X_BF16_SUBLANES, V7X_LANES
        h = h_next[0]
        every = functools.reduce(lambda a, b: a + b, [h[r:r + rows] for r in range(0, h.shape[0], rows)])
        every = functools.reduce(lambda a, b: a + b, [every[:, c:c + lanes] for c in range(0, h.shape[1], lanes)])
        done = _zero_after(every.astype(F32)).astype(BF16)
        act_ref[0:rows, 0:lanes] = act_ref[0:rows, 0:lanes] + done

    side_work[n_in - 1].append(norm_next)
    side_work[n_in + NORM_NEXT_OUT_CHUNKS - 1].append(norm_next_deadline)
    out = _ffn_body(h_ref, x_ref[...], win_ref, wout_ref, act_ref, side_work)
    o_ref[...] = _rmsnorm(out, og_ref[...]) if out_norm else out


def _ffn(x2d, norm_g, w_in, w_out, out_g, layer, which, out_norm, next_ffn):
    n_tok, d = x2d.shape
    d_ff = w_out.shape[0]
    tm = FFN_TOKEN_TILE
    assert n_tok % tm == 0 and d_ff % FFN_HIDDEN_CHUNK == 0
    n_steps = n_tok // tm
    cast_in_specs, cast_out_specs, cast_out_shapes = (
        _cast_ahead_plumbing(next_ffn, n_steps, lambda t: t) if next_ffn else ([], [], []))
    cast_bytes = 3 * (w_in.size + w_out.size) * 2 * 2 // n_steps
    resident = ((3 * 2 * tm * d * 4) + (w_in.size + w_out.size) * 2 + tm * d_ff * 2 + tm * d * 2 + cast_bytes)
    outs = pl.pallas_call(
        functools.partial(_ffn_kernel, out_norm=out_norm, cast_next=bool(next_ffn)),
        grid=(n_steps,),
        in_specs=[
            pl.BlockSpec((tm, d), lambda t: (t, 0)),
            pl.BlockSpec((tm, d), lambda t: (jnp.minimum(t + 1, n_steps - 1), 0)),
            pl.BlockSpec((None, 1, d), _const_index(3 * layer + 2 * which, 0, 0)),
            pl.BlockSpec((d, 2 * d_ff), _const_index(0, 0), pipeline_mode=pl.Buffered(1)),
            pl.BlockSpec((d_ff, d), _const_index(0, 0), pipeline_mode=pl.Buffered(1)),
            pl.BlockSpec((1, d), _const_index(0, 0)),
            *cast_in_specs,
        ],
        out_specs=[pl.BlockSpec((tm, d), lambda t: (t, 0)), *cast_out_specs],
        out_shape=[jax.ShapeDtypeStruct((n_tok, d), x2d.dtype), *cast_out_shapes],
        scratch_shapes=[pltpu.VMEM((tm, d), BF16), pltpu.VMEM((tm, d_ff), BF16)],
        compiler_params=pltpu.CompilerParams(
            dimension_semantics=("arbitrary",), vmem_limit_bytes=_vmem_limit(resident)),
        name=f"ffn_l{layer}_{which}",
    )(x2d, x2d, norm_g, w_in, w_out, out_g, *(next_ffn[:2] if next_ffn else ()))
    return tuple(outs) if next_ffn else (outs[0], None, None)


def _pool_ffn_kernel(xp_ref, x_ref, xn_ref, gm_ref, w_ref, b_ref, s_ref, gf_ref, win_ref, wout_ref, og_ref,
                     *rest, seq_len, tiles_per_seq, out_norm, cast_next):
    cast_work = []
    if cast_next:
        *next_f32_refs, o_ref, next_in_ref, next_out_ref = rest[:5]
        cast_work.append(functools.partial(_cast_ahead, next_f32_refs, (next_in_ref, next_out_ref)))
        rest = rest[5:]
    else:
        o_ref, rest = rest[0], rest[1:]
    xmid_ref, hmid_ref, xres_ref, act_ref, h_ref, pooled_ref = rest
    tile, d = x_ref.shape
    halo = POOL_HALO
    lw = V7X_LANES
    gd = w_ref.shape[-1]
    rc = POOL_ROW_CHUNK
    t = pl.program_id(0)
    n_tiles = pl.num_programs(0) - 1
    i = lax.rem(jnp.minimum(t, n_tiles - 1), tiles_per_seq)

    @pl.when(t == 0)
    def _():
        xmid_ref[...] = jnp.zeros(xmid_ref.shape, F32)
        hmid_ref[...] = jnp.zeros(hmid_ref.shape, BF16)

    def norm_to_slabs(sync):
        del sync
        g = gm_ref[...]
        _store_slabs(h_ref, 0, ((0, jnp.where(i == 0, 0.0, _rmsnorm(xp_ref[...], g))),
                                (halo, _rmsnorm(x_ref[...], g)),
                                (halo + tile, jnp.where(i == tiles_per_seq - 1, 0.0, _rmsnorm(xn_ref[...], g)))))

    def pool_rows(r0, sync):
        sync_rows = jnp.tile(sync, (rc // V7X_SUBLANES, 1))
        pos = i * tile + r0 + lax.broadcasted_iota(jnp.int32, (rc, 1), 0)
        for gi, w in enumerate(POOL_WINDOWS):
            lo = pos - w // 2
            cnt = (jnp.minimum(lo + w, seq_len) - jnp.maximum(lo, 0)).astype(F32)
            inv_cnt = 1.0 / cnt
            for slab in range(gi * gd // lw, (gi + 1) * gd // lw):
                win_sum = _load_slab_rows(h_ref, slab, halo + r0 - w // 2, rc) + sync_rows
                for k in range(1, w):
                    win_sum = win_sum + _load_slab_rows(h_ref, slab, halo + r0 - w // 2 + k, rc)
                pooled = win_sum * inv_cnt - _load_slab_rows(h_ref, slab, halo + r0, rc)
                pooled_ref[r0:r0 + rc, slab * lw:(slab + 1) * lw] = pooled.astype(BF16)

    def mix_rows(r0, sync):
        del sync
        for gi in range(len(POOL_WINDOWS)):
            lanes = slice(gi * gd, (gi + 1) * gd)
            y = (jnp.dot(pooled_ref[r0:r0 + rc, lanes], w_ref[gi], preferred_element_type=F32)
                 + b_ref[:, lanes])
            xmid_ref[r0:r0 + rc, lanes] = x_ref[r0:r0 + rc, lanes] + y * s_ref[:, lanes]

    def norm_mixed(sync):
        del sync
        hmid_ref[...] = _rmsnorm(xmid_ref[...], gf_ref[...]).astype(BF16)

    n_in, n_slots = _ffn_slots(d, wout_ref.shape[0])
    row_chunks = list(range(0, tile, rc))
    spacing = (n_in - 2 - SIDE_WORK_LAG) // len(row_chunks)
    assert spacing >= 1
    side_work = {slot: [] for slot in range(n_slots)}
    side_work[0].append(norm_to_slabs)
    for k, r0 in enumerate(row_chunks):
        side_work[1 + k * spacing].append(functools.partial(pool_rows, r0))
        side_work[1 + k * spacing + SIDE_WORK_LAG].append(functools.partial(mix_rows, r0))
    side_work[n_in - 1].append(norm_mixed)
    side_work[n_in].extend(cast_work)

    xres_ref[...] = xmid_ref[...]
    out = _ffn_body(hmid_ref[...], xres_ref[...], win_ref, wout_ref, act_ref, side_work)
    o_ref[...] = _rmsnorm(out, og_ref[...]) if out_norm else out


def _pool_ffn(x, norm_g, pool_w, pool_b, pool_scale, w_in, w_out, out_g, layer, j, out_norm, next_ffn):
    b, s, d = x.shape
    d_ff = w_out.shape[0]
    tile = POOL_FFN_TOKEN_TILE
    halo = POOL_HALO
    assert s % tile == 0 and tile % halo == 0 and tile % POOL_ROW_CHUNK == 0 and d_ff % FFN_HIDDEN_CHUNK == 0
    assert d % (SLABS_PER_PLANE * V7X_LANES) == 0 and pool_w.shape[-1] % V7X_LANES == 0
    assert max(POOL_WINDOWS) // 2 <= halo
    tiles_per_seq = s // tile
    n_tiles = b * tiles_per_seq

    def mixer_tile(t):
        tb = jnp.minimum(t, n_tiles - 1)
        return tb // tiles_per_seq, tb % tiles_per_seq

    def ffn_tile(t):
        ta = jnp.maximum(t - 1, 0)
        return ta // tiles_per_seq, ta % tiles_per_seq, 0

    prev_halo, cur, next_halo = _halo_index_maps(tile, halo, s, mixer_tile)
    cast_in_specs, cast_out_specs, cast_out_shapes = (
        _cast_ahead_plumbing(next_ffn, n_tiles, lambda t: jnp.minimum(t, n_tiles - 1)) if next_ffn
        else ([], [], []))
    cast_bytes = 3 * (w_in.size + w_out.size) * 2 * 2 // n_tiles
    resident = ((w_in.size + w_out.size) * 2 + 2 * pool_w[0].size * 2 + 2 * 2 * tile * d * 4
                + 2 * tile * d * 4 + 2 * tile * d * 2 + tile * d_ff * 2 + (tile + 2 * halo) * d * 4 + cast_bytes)
    outs = pl.pallas_call(
        functools.partial(_pool_ffn_kernel, seq_len=s, tiles_per_seq=tiles_per_seq, out_norm=out_norm,
                          cast_next=bool(next_ffn)),
        grid=(n_tiles + 1,),
        in_specs=[
            pl.BlockSpec((None, halo, d), prev_halo),
            pl.BlockSpec((None, tile, d), cur),
            pl.BlockSpec((None, halo, d), next_halo),
            pl.BlockSpec((None, 1, d), _const_index(3 * layer + 1, 0, 0)),
            pl.BlockSpec((None,) + pool_w.shape[1:], _const_index(j, 0, 0, 0)),
            pl.BlockSpec((None, 1, d), _const_index(j, 0, 0)),
            pl.BlockSpec((None, 1, d), _const_index(j, 0, 0)),
            pl.BlockSpec((None, 1, d), _const_index(3 * layer + 2, 0, 0)),
            pl.BlockSpec((d, 2 * d_ff), _const_index(0, 0), pipeline_mode=pl.Buffered(1)),
            pl.BlockSpec((d_ff, d), _const_index(0, 0), pipeline_mode=pl.Buffered(1)),
            pl.BlockSpec((1, d), _const_index(0, 0)),
            *cast_in_specs,
        ],
        out_specs=[pl.BlockSpec((None, tile, d), ffn_tile), *cast_out_specs],
        out_shape=[jax.ShapeDtypeStruct(x.shape, x.dtype), *cast_out_shapes],
        scratch_shapes=[
            pltpu.VMEM((tile, d), F32),
            pltpu.VMEM((tile, d), BF16),
            pltpu.VMEM((tile, d), F32),
            pltpu.VMEM((tile, d_ff), BF16),
            pltpu.VMEM(_slab_plane_shape(tile + 2 * halo, d), F32),
            pltpu.VMEM((tile, d), BF16),
        ],
        compiler_params=pltpu.CompilerParams(
            dimension_semantics=("arbitrary",), vmem_limit_bytes=_vmem_limit(resident)),
        name=f"pool_ffn_l{layer}",
    )(x, x, x, norm_g, pool_w, pool_b, pool_scale, norm_g, w_in, w_out, out_g,
      *(next_ffn[:2] if next_ffn else ()))
    return tuple(outs) if next_ffn else (outs[0], None, None)


def _conv_kernel(xp_ref, x_ref, xn_ref, g_ref, w1f_ref, b1_ref, wdw_ref, bdw_ref, lng_ref, lnb_ref,
                 w2f_ref, b2_ref, o_ref, h_ref, glu_ref, conv_ref, act_ref, w1_ref, w2_ref):
    tile, d = x_ref.shape
    n_taps = wdw_ref.shape[0]
    i = pl.program_id(1)
    last = pl.num_programs(1) - 1

    @pl.when(jnp.logical_and(pl.program_id(0) == 0, i == 0))
    def _():
        w1_ref[...] = w1f_ref[...].astype(BF16)
        w2_ref[...] = w2f_ref[...].astype(BF16)

    g = g_ref[...]
    halo = CONV_HALO
    lw = V7X_LANES
    spp = SLABS_PER_PLANE
    h_ref[0:halo, :] = _rmsnorm(xp_ref[...], g).astype(BF16)
    h_ref[halo:halo + tile, :] = _rmsnorm(x_ref[...], g).astype(BF16)
    h_ref[halo + tile:, :] = _rmsnorm(xn_ref[...], g).astype(BF16)

    h = h_ref[...]
    for c in range(d // (2 * lw)):
        lo = 2 * lw * c
        a = jnp.dot(h, w1_ref[:, lo:lo + 2 * lw], preferred_element_type=F32) + b1_ref[:, lo:lo + 2 * lw]
        gate = (jnp.dot(h, w1_ref[:, d + lo:d + lo + 2 * lw], preferred_element_type=F32)
                + b1_ref[:, d + lo:d + lo + 2 * lw])
        glu = a * jax.nn.sigmoid(gate)
        _store_slabs(glu_ref, 2 * c, ((0, jnp.where(i == 0, 0.0, glu[:halo])),
                                      (halo, glu[halo:halo + tile]),
                                      (halo + tile, jnp.where(i == last, 0.0, glu[halo + tile:]))))

    first = halo - n_taps // 2
    rc = CONV_ROW_CHUNK
    sub = V7X_SUBLANES
    for plane in range(d // (spp * lw)):
        slab_lanes = [slice((plane * spp + p) * lw, (plane * spp + p + 1) * lw) for p in range(spp)]
        for r0 in range(0, tile, rc):
            acc = [jnp.broadcast_to(bdw_ref[:, slab_lanes[p]], (rc // sub, sub, lw)) for p in range(spp)]
            for k in range(n_taps):
                for p in range(spp):
                    window = _load_slab_rows(glu_ref, plane * spp + p, r0 + first + k, rc)
                    tap = jnp.broadcast_to(wdw_ref[k:k + 1, slab_lanes[p]], (sub, lw))
                    acc[p] = acc[p] + window.reshape(rc // sub, sub, lw) * tap
            for p in range(spp):
                conv_ref[r0:r0 + rc, slab_lanes[p]] = acc[p].reshape(rc, lw)

    lc = LN_ROW_CHUNK
    for r0 in range(0, tile, lc):
        v = conv_ref[r0:r0 + lc, :]
        mu = jnp.mean(v, axis=-1, keepdims=True)
        xc = v - mu
        var = jnp.mean(xc * xc, axis=-1, keepdims=True)
        v = xc * lax.rsqrt(var + LN_EPS) * lng_ref[...] + lnb_ref[...]
        act_ref[r0:r0 + lc, :] = (v * jax.nn.sigmoid(v)).astype(BF16)
    y = jnp.dot(act_ref[...], w2_ref[...], preferred_element_type=F32) + b2_ref[...]
    o_ref[...] = x_ref[...] + y


def _conv_mixer(x, norm_g, w_pw1, b_pw1, w_dw, b_dw, ln_g, ln_b, w_pw2, b_pw2, layer, j):
    b, s, d = x.shape
    tile = CONV_TOKEN_TILE
    halo = CONV_HALO
    n_taps = w_dw.shape[1]
    assert s % tile == 0 and tile % halo == 0 and n_taps // 2 <= halo
    assert tile % CONV_ROW_CHUNK == 0 and tile % LN_ROW_CHUNK == 0
    assert d % (SLABS_PER_PLANE * V7X_LANES) == 0 and SLABS_PER_PLANE % 2 == 0
    prev_halo, cur, next_halo = _halo_index_maps(tile, halo, s, lambda bi, i: (bi, i))
    row_spec = lambda width: pl.BlockSpec((None, 1, width), _const_index(j, 0, 0))
    resident = ((w_pw1[0].size + w_pw2[0].size) * (4 + 2) + 2 * 2 * tile * d * 4
                + (tile + 2 * halo) * d * (2 + 4) + tile * d * (4 + 2))
    return pl.pallas_call(
        _conv_kernel,
        grid=(b, s // tile),
        in_specs=[
            pl.BlockSpec((None, halo, d), prev_halo),
            pl.BlockSpec((None, tile, d), cur),
            pl.BlockSpec((None, halo, d), next_halo),
            pl.BlockSpec((None, 1, d), _const_index(3 * layer + 1, 0, 0)),
            pl.BlockSpec((None, d, 2 * d), _const_index(j, 0, 0), pipeline_mode=pl.Buffered(1)),
            row_spec(2 * d),
            pl.BlockSpec((None, n_taps, d), _const_index(j, 0, 0)),
            row_spec(d),
            row_spec(d),
            row_spec(d),
            pl.BlockSpec((None, d, d), _const_index(j, 0, 0), pipeline_mode=pl.Buffered(1)),
            row_spec(d),
        ],
        out_specs=pl.BlockSpec((None, tile, d), cur),
        out_shape=jax.ShapeDtypeStruct(x.shape, x.dtype),
        scratch_shapes=[
            pltpu.VMEM((tile + 2 * halo, d), BF16),
            pltpu.VMEM(_slab_plane_shape(tile + 2 * halo, d), F32),
            pltpu.VMEM((tile, d), F32),
            pltpu.VMEM((tile, d), BF16),
            pltpu.VMEM(w_pw1.shape[1:], BF16),
            pltpu.VMEM(w_pw2.shape[1:], BF16),
        ],
        compiler_params=pltpu.CompilerParams(
            dimension_semantics=("arbitrary", "arbitrary"), vmem_limit_bytes=_vmem_limit(resident)),
        name=f"conv_mixer_l{layer}",
    )(x, x, x, norm_g, w_pw1, b_pw1, w_dw, b_dw, ln_g, ln_b, w_pw2, b_pw2)


def kernel(x, norm_g, ffn_w_in, ffn_w_out, pool_w, pool_b, pool_scale, conv_w_pw1, conv_b_pw1, conv_w_dw,
           conv_b_dw, conv_ln_g, conv_ln_b, conv_w_pw2, conv_b_pw2, final_g):
    b, s, d = x.shape
    depth = norm_g.shape[0]
    row3 = lambda a: a.reshape(a.shape[0], 1, a.shape[1])
    norm_rows = norm_g.reshape(depth * 3, 1, d)
    pool_w16 = pool_w.astype(BF16)
    out_g = final_g.reshape(1, d)
    assert depth >= 1

    w_in, w_out = ffn_w_in[0, 0].astype(BF16), ffn_w_out[0, 0].astype(BF16)

    def next_ffn(layer, which):
        layer, which = (layer, 1) if which == 0 else (layer + 1, 0)
        return (ffn_w_in, ffn_w_out, layer, which) if layer < depth else None

    for i in range(depth):
        x, w_in, w_out = _ffn(x.reshape(b * s, d), norm_rows, w_in, w_out, out_g, i, 0, False, next_ffn(i, 0))
        x = x.reshape(b, s, d)
        j = i // N_MIXERS
        last_layer = i == depth - 1
        if i % N_MIXERS == 0:
            x, w_in, w_out = _pool_ffn(x, norm_rows, pool_w16, row3(pool_b), row3(pool_scale), w_in, w_out,
                                       out_g, i, j, last_layer, next_ffn(i, 1))
        else:
            x = _conv_mixer(x, norm_rows, conv_w_pw1, row3(conv_b_pw1), conv_w_dw, row3(conv_b_dw),
                            row3(conv_ln_g), row3(conv_ln_b), conv_w_pw2, row3(conv_b_pw2), i, j)
            x, w_in, w_out = _ffn(x.reshape(b * s, d), norm_rows, w_in, w_out, out_g, i, 1, last_layer,
                                  next_ffn(i, 1))
            x = x.reshape(b, s, d)
    return x
```
